```python
import jax, jax.numpy as jnp
from jax import lax
import numpy as np

D_MODEL = 1024
BATCH = 2
SEQ = 8192
DEPTH = 4
DEC_BATCH = 128
DEC_SEQ = 1
PAST_LEN = 2048
PAGE_SIZE = 128

N_MIXERS = 2
N_A_LAYERS = (DEPTH + 1) // 2
N_B_LAYERS = DEPTH // 2

WINDOWS = (128, 512, 2048)
DILATIONS = (1, 4, 16)
N_GROUPS = 3
A_HEADS = 8
A_HEAD_DIM = 64
A_BLOCK = 128
R_HEADS = 4
R_QK_DIM = D_MODEL // R_HEADS
R_V_DIM = 2 * R_QK_DIM
R_CHUNK = 128
FFN_HIDDEN = ((8 * D_MODEL + 3 * 256 - 1) // (3 * 256)) * 256
ROPE_THETA = 10000.0
EPS = 1e-6

kernel_name = "dilated_window_retention_hybrid_step"


def rmsnorm(x, g):
    xf = x.astype(jnp.float32)
    y = xf * lax.rsqrt(jnp.mean(xf * xf, axis=-1, keepdims=True) + EPS)
    return (y * g.astype(jnp.float32)).astype(x.dtype)


def rope(x, pos):
    half = x.shape[-1] // 2
    inv = ROPE_THETA ** (-jnp.arange(half, dtype=jnp.float32) / half)
    ang = pos.astype(jnp.float32)[:, None] * inv[None, :]
    shp = (pos.shape[0],) + (1,) * (x.ndim - 3) + (half,)
    cos = jnp.cos(ang).reshape(shp)
    sin = jnp.sin(ang).reshape(shp)
    xf = x.astype(jnp.float32)
    x1, x2 = xf[..., :half], xf[..., half:]
    return jnp.concatenate([x1 * cos - x2 * sin, x2 * cos + x1 * sin], axis=-1).astype(x.dtype)


def swiglu(h, w_gate, w_up, w_down):
    return (jax.nn.silu(h @ w_gate) * (h @ w_up)) @ w_down


def a_project(h, pos, w_qkv, q_norm, k_norm):
    B, S, _ = h.shape
    p = (h @ w_qkv).reshape(B, S, N_GROUPS, 3, A_HEADS, A_HEAD_DIM)
    q = rmsnorm(p[:, :, :, 0], q_norm[:, None, :])
    k = rmsnorm(p[:, :, :, 1], k_norm[:, None, :])
    q = rope(q, pos) * (A_HEAD_DIM ** -0.5)
    k = rope(k, pos)
    return q, k, p[:, :, :, 2]


def dilated_band_attention(q, k, v, dil, n_rel):
    B, S, H, Dh = q.shape
    L = S // dil
    N = B * dil

    def to_sub(t):
        return t.reshape(B, L, dil, H, Dh).transpose(0, 2, 1, 3, 4).reshape(N, L, H, Dh)

    nb = -(-L // A_BLOCK)
    pad = nb * A_BLOCK - L
    qb = jnp.pad(to_sub(q), ((0, 0), (0, pad), (0, 0), (0, 0))).reshape(N, nb, A_BLOCK, H, Dh)

    def band(t):
        tb = jnp.pad(to_sub(t), ((0, 0), (A_BLOCK, pad), (0, 0), (0, 0))).reshape(N, nb + 1, A_BLOCK, H, Dh)
        return jnp.concatenate([tb[:, :-1], tb[:, 1:]], axis=2)

    kb, vb = band(k), band(v)
    s = jnp.einsum('nbqhd,nbkhd->nbhqk', qb, kb, preferred_element_type=jnp.float32)
    bi = jnp.arange(nb)[:, None, None]
    qi = jnp.arange(A_BLOCK)[None, :, None]
    kj = jnp.arange(2 * A_BLOCK)[None, None, :]
    dist = A_BLOCK + qi - kj
    kpos = bi * A_BLOCK + kj - A_BLOCK
    valid = (dist >= 0) & (dist <= n_rel) & (kpos >= 0)
    s = jnp.where(valid[:, None], s, -jnp.inf)
    m = jnp.max(s, axis=-1)
    p = jnp.exp(s - m[..., None])
    l = jnp.sum(p, axis=-1)
    acc = jnp.einsum('nbhqk,nbkhd->nbqhd', p, vb.astype(jnp.float32))

    def from_sub(t):
        t = t.reshape((N, nb * A_BLOCK) + t.shape[3:])[:, :L]
        tail = t.shape[2:]
        t = t.reshape((B, dil, L) + tail)
        return jnp.swapaxes(t, 1, 2).reshape((B, S) + tail)

    return from_sub(acc), from_sub(jnp.swapaxes(m, 2, 3)), from_sub(jnp.swapaxes(l, 2, 3))


def combine_groups(accs, ms, ls):
    m_all = jnp.stack(ms)
    w = jnp.exp(m_all - jnp.max(m_all, axis=0))
    num = jnp.einsum('gbshd,gbsh->bshd', jnp.stack(accs), w)
    den = jnp.sum(jnp.stack(ls) * w, axis=0)
    return num / den[..., None]


def mixer_a_prompt(h, w_qkv, q_norm, k_norm, w_o):
    B, S, _ = h.shape
    q, k, v = a_project(h, jnp.arange(S), w_qkv, q_norm, k_norm)
    accs, ms, ls, rows = [], [], [], []
    for g in range(N_GROUPS):
        acc, m, l = dilated_band_attention(q[:, :, g], k[:, :, g], v[:, :, g],
                                           DILATIONS[g], WINDOWS[g] // DILATIONS[g])
        accs.append(acc); ms.append(m); ls.append(l)
        keep = min(WINDOWS[g], S)
        rows.append(jnp.stack([k[:, :, g], v[:, :, g]], axis=2)[:, S - keep:])
    o = combine_groups(accs, ms, ls).astype(h.dtype).reshape(B, S, A_HEADS * A_HEAD_DIM)
    return o @ w_o, rows


def mixer_a_sample(h, bufs, w_qkv, q_norm, k_norm, w_o):
    B, T, _ = h.shape
    q, k, v = a_project(h, PAST_LEN + jnp.arange(T), w_qkv, q_norm, k_norm)
    accs, ms, ls, new_bufs = [], [], [], []
    for g in range(N_GROUPS):
        buf = bufs[g]
        Lg = buf.shape[1]
        cat = jnp.concatenate([buf, jnp.stack([k[:, :, g], v[:, :, g]], axis=2).astype(buf.dtype)], axis=1)
        n_rel = WINDOWS[g] // DILATIONS[g]
        ridx = Lg + jnp.arange(T)[:, None] - DILATIONS[g] * jnp.arange(n_rel + 1)[None, :]
        kv = jnp.take(cat, jnp.maximum(ridx, 0), axis=1)
        s = jnp.einsum('bthd,btjhd->bthj', q[:, :, g], kv[:, :, :, 0], preferred_element_type=jnp.float32)
        s = jnp.where((ridx >= 0)[None, :, None, :], s, -jnp.inf)
        m = jnp.max(s, axis=-1)
        p = jnp.exp(s - m[..., None])
        accs.append(jnp.einsum('bthj,btjhd->bthd', p, kv[:, :, :, 1].astype(jnp.float32)))
        ms.append(m); ls.append(jnp.sum(p, axis=-1))
        new_bufs.append(cat[:, T:])
    o = combine_groups(accs, ms, ls).astype(h.dtype).reshape(B, T, A_HEADS * A_HEAD_DIM)
    return o @ w_o, new_bufs


def retention_log_decay():
    return jnp.log1p(-jnp.exp2(-5.0 - jnp.arange(R_HEADS, dtype=jnp.float32)))


def retention_chunk(state, q, k, v, log_decay):
    C = q.shape[2]
    idx = jnp.arange(C, dtype=jnp.float32)
    rel = idx[:, None] - idx[None, :]
    dmask = jnp.where(rel >= 0, jnp.exp(jnp.maximum(rel, 0.0) * log_decay[:, None, None]), 0.0)
    q_decay = jnp.exp((idx + 1.0)[None, :] * log_decay[:, None])
    k_decay = jnp.exp((C - 1.0 - idx)[None, :] * log_decay[:, None])
    inter = jnp.einsum('bhcd,bhde->bhce', q, state) * q_decay[None, :, :, None]
    scores = jnp.einsum('bhid,bhjd->bhij', q, k) * dmask[None]
    o = inter + jnp.einsum('bhij,bhje->bhie', scores, v)
    new_state = state * jnp.exp(C * log_decay)[None, :, None, None] + \
        jnp.einsum('bhjd,bhje->bhde', k * k_decay[None, :, :, None], v)
    return o, new_state


def r_project(h, pos, w_in):
    B, S, _ = h.shape
    p = h @ w_in
    nqk, nv = R_HEADS * R_QK_DIM, R_HEADS * R_V_DIM
    q = p[..., :nqk].reshape(B, S, R_HEADS, R_QK_DIM)
    k = p[..., nqk:2 * nqk].reshape(B, S, R_HEADS, R_QK_DIM)
    v = p[..., 2 * nqk:2 * nqk + nv].reshape(B, S, R_HEADS, R_V_DIM)
    gate = p[..., 2 * nqk + nv:]
    q = rope(q, pos)
    k = rope(k, pos) * (R_QK_DIM ** -0.5)
    return q, k, v, gate


def r_output(o, gate, out_norm, w_o):
    B, S = o.shape[:2]
    y = rmsnorm(o, out_norm).reshape(B, S, R_HEADS * R_V_DIM)
    y = y * jax.nn.silu(gate.astype(jnp.float32))
    return y.astype(gate.dtype) @ w_o


def mixer_b_prompt(h, w_in, out_norm, w_o):
    B, S, _ = h.shape
    q, k, v, gate = r_project(h, jnp.arange(S), w_in)
    lg = retention_log_decay()
    nc = S // R_CHUNK

    def chunks(t):
        return t.astype(jnp.float32).reshape(B, nc, R_CHUNK, R_HEADS, t.shape[-1]).transpose(1, 0, 3, 2, 4)

    def step(state, qkv):
        o, st = retention_chunk(state, qkv[0], qkv[1], qkv[2], lg)
        return st, o

    s0 = jnp.zeros((B, R_HEADS, R_QK_DIM, R_V_DIM), jnp.float32)
    s_final, o = lax.scan(step, s0, (chunks(q), chunks(k), chunks(v)))
    o = o.transpose(1, 0, 3, 2, 4).reshape(B, S, R_HEADS, R_V_DIM)
    return r_output(o, gate, out_norm, w_o), s_final


def mixer_b_sample(h, state, w_in, out_norm, w_o):
    B, T, _ = h.shape
    q, k, v, gate = r_project(h, PAST_LEN + jnp.arange(T), w_in)

    def heads_first(t):
        return t.astype(jnp.float32).transpose(0, 2, 1, 3)

    o, new_state = retention_chunk(state.astype(jnp.float32), heads_first(q), heads_first(k),
                                   heads_first(v), retention_log_decay())
    return r_output(o.transpose(0, 2, 1, 3), gate, out_norm, w_o), new_state


def setup_inputs(seed: int = 0) -> dict:
    key = jax.random.key(seed)
    ks = jax.random.split(key, 24)
    f32 = jnp.float32

    def nrm(k, shape, scale):
        return scale * jax.random.normal(k, shape, f32)

    a_width = A_HEADS * A_HEAD_DIM
    r_in = R_HEADS * (2 * R_QK_DIM + 2 * R_V_DIM)
    return {
        "x_prompt": nrm(ks[0], (BATCH, SEQ, D_MODEL), 1.0),
        "x_sample": nrm(ks[1], (DEC_BATCH, DEC_SEQ, D_MODEL), 1.0),
        "cache_kv_g0": nrm(ks[2], (N_A_LAYERS, DEC_BATCH, min(WINDOWS[0], PAST_LEN), 2, A_HEADS, A_HEAD_DIM), 1.0),
        "cache_kv_g1": nrm(ks[3], (N_A_LAYERS, DEC_BATCH, min(WINDOWS[1], PAST_LEN), 2, A_HEADS, A_HEAD_DIM), 1.0),
        "cache_kv_g2": nrm(ks[4], (N_A_LAYERS, DEC_BATCH, min(WINDOWS[2], PAST_LEN), 2, A_HEADS, A_HEAD_DIM), 1.0),
        "state_retention": nrm(ks[5], (N_B_LAYERS, DEC_BATCH, R_HEADS, R_QK_DIM, R_V_DIM), 1.0),
        "mix_norm": 1.0 + nrm(ks[6], (DEPTH, D_MODEL), 0.1),
        "ffn_norm": 1.0 + nrm(ks[7], (DEPTH, D_MODEL), 0.1),
        "a_w_qkv": nrm(ks[8], (N_A_LAYERS, D_MODEL, N_GROUPS * 3 * a_width), D_MODEL ** -0.5),
        "a_q_norm": 1.0 + nrm(ks[9], (N_A_LAYERS, N_GROUPS, A_HEAD_DIM), 0.1),
        "a_k_norm": 1.0 + nrm(ks[10], (N_A_LAYERS, N_GROUPS, A_HEAD_DIM), 0.1),
        "a_w_o": nrm(ks[11], (N_A_LAYERS, a_width, D_MODEL), a_width ** -0.5),
        "r_w_in": nrm(ks[12], (N_B_LAYERS, D_MODEL, r_in), D_MODEL ** -0.5),
        "r_out_norm": 1.0 + nrm(ks[13], (N_B_LAYERS, R_HEADS, R_V_DIM), 0.1),
        "r_w_o": nrm(ks[14], (N_B_LAYERS, R_HEADS * R_V_DIM, D_MODEL), (R_HEADS * R_V_DIM) ** -0.5),
        "f_w_gate": nrm(ks[15], (DEPTH, D_MODEL, FFN_HIDDEN), D_MODEL ** -0.5),
        "f_w_up": nrm(ks[16], (DEPTH, D_MODEL, FFN_HIDDEN), D_MODEL ** -0.5),
        "f_w_down": nrm(ks[17], (DEPTH, FFN_HIDDEN, D_MODEL), FFN_HIDDEN ** -0.5),
    }


def reference(x_prompt, x_sample, cache_kv_g0, cache_kv_g1, cache_kv_g2, state_retention,
              mix_norm, ffn_norm, a_w_qkv, a_q_norm, a_k_norm, a_w_o,
              r_w_in, r_out_norm, r_w_o, f_w_gate, f_w_up, f_w_down):
    xp, xs = x_prompt, x_sample
    caches = (cache_kv_g0, cache_kv_g1, cache_kv_g2)
    kv_p = [[] for _ in range(N_GROUPS)]
    kv_s = [[] for _ in range(N_GROUPS)]
    ret_p, ret_s = [], []
    for i in range(DEPTH):
        hp = rmsnorm(xp, mix_norm[i])
        hs = rmsnorm(xs, mix_norm[i])
        j = i // N_MIXERS
        if i % N_MIXERS == 0:
            yp, rows = mixer_a_prompt(hp, a_w_qkv[j], a_q_norm[j], a_k_norm[j], a_w_o[j])
            ys, bufs = mixer_a_sample(hs, [c[j] for c in caches], a_w_qkv[j], a_q_norm[j], a_k_norm[j], a_w_o[j])
            for g in range(N_GROUPS):
                kv_p[g].append(rows[g])
                kv_s[g].append(bufs[g])
        else:
            yp, sp = mixer_b_prompt(hp, r_w_in[j], r_out_norm[j], r_w_o[j])
            ys, ss = mixer_b_sample(hs, state_retention[j], r_w_in[j], r_out_norm[j], r_w_o[j])
            ret_p.append(sp)
            ret_s.append(ss)
        xp = xp + yp
        xs = xs + ys
        xp = xp + swiglu(rmsnorm(xp, ffn_norm[i]), f_w_gate[i], f_w_up[i], f_w_down[i])
        xs = xs + swiglu(rmsnorm(xs, ffn_norm[i]), f_w_gate[i], f_w_up[i], f_w_down[i])
    return (xp, xs,
            jnp.stack(kv_p[0]), jnp.stack(kv_s[0]),
            jnp.stack(kv_p[1]), jnp.stack(kv_s[1]),
            jnp.stack(kv_p[2]), jnp.stack(kv_s[2]),
            jnp.stack(ret_p), jnp.stack(ret_s))
```

```python
import functools
import math

import jax
import jax.numpy as jnp
from jax import lax
from jax.experimental import pallas as pl
from jax.experimental.pallas import tpu as pltpu

F32 = jnp.float32
BF16 = jnp.bfloat16

EPS = 1e-6
ROPE_THETA = 10000.0
WINDOWS = (128, 512, 2048)
DILATIONS = (1, 4, 16)
N_GROUPS = 3
A_HEADS = 8
A_HEAD_DIM = 64
A_WIDTH = A_HEADS * A_HEAD_DIM
A_BLOCK = 128
R_HEADS = 4
R_CHUNK = 128
PAST_LEN = 2048
NEG = -1e30

VMEM_LIMIT_BYTES = 56 * 1024 * 1024
LANES = 128
MXU_DIM = 256


def _cparams(*sem):
    return pltpu.CompilerParams(dimension_semantics=sem, vmem_limit_bytes=VMEM_LIMIT_BYTES)


def _resident(shape, index_map):
    return pl.BlockSpec(shape, index_map, pipeline_mode=pl.Buffered(1))


def _rms(x, g):
    ms = jnp.mean(x * x, axis=-1, keepdims=True)
    return x * lax.rsqrt(ms + EPS) * g


def _silu(x):
    return x * (1.0 / (1.0 + jnp.exp(-x)))


def _dot(a, b):
    return jnp.dot(a, b, preferred_element_type=F32)


def _dot_nt(a, b):
    return lax.dot_general(a, b, (((1,), (1,)), ((), ())), preferred_element_type=F32)


def _dot_tn(a, b):
    return lax.dot_general(a, b, (((0,), (0,)), ((), ())), preferred_element_type=F32)


def _a_proj_kernel(x_ref, gmix_ref, w_ref, gq_ref, gk_ref, cos_ref, sin_ref, ones_ref,
                   q_ref, k_ref, v_ref, kv_ref):
    h = _rms(x_ref[...], gmix_ref[...]).astype(BF16)
    cos = jnp.concatenate([cos_ref[...]] * (A_WIDTH // LANES), axis=1)
    sin = jnp.concatenate([sin_ref[...]] * (A_WIDTH // LANES), axis=1)
    lane = lax.broadcasted_iota(jnp.int32, (1, A_WIDTH), 1)
    first_half = (lane % A_HEAD_DIM) < (A_HEAD_DIM // 2)
    ones = ones_ref[...]

    def normed_rope(p, gain):
        y = (p * p).astype(BF16)
        ss = jnp.concatenate([_dot(y[:, :MXU_DIM], ones), _dot(y[:, MXU_DIM:], ones)], axis=1)
        pn = p * lax.rsqrt(ss * (1.0 / A_HEAD_DIM) + EPS) * gain
        rot = jnp.where(first_half,
                        pltpu.roll(pn, A_WIDTH - A_HEAD_DIM // 2, 1),
                        pltpu.roll(pn, A_HEAD_DIM // 2, 1))
        return pn * cos + rot * sin

    q = normed_rope(_dot(h, w_ref[:, 0:A_WIDTH]), gq_ref[...]) * (A_HEAD_DIM ** -0.5)
    q_ref[...] = q.astype(BF16)
    k = normed_rope(_dot(h, w_ref[:, A_WIDTH:2 * A_WIDTH]), gk_ref[...])
    k_ref[...] = k.astype(BF16)
    v = _dot(h, w_ref[:, 2 * A_WIDTH:3 * A_WIDTH])
    v_ref[...] = v.astype(BF16)
    kv_ref[:, 0:A_WIDTH] = k
    kv_ref[:, A_WIDTH:2 * A_WIDTH] = v


def _a_proj(x, gmix, w, gq, gk, cos, sin, ones, *, tm, seq, tail):
    m, d = x.shape
    mt = m // tm
    tps, tpt = seq // tm, tail // tm
    ntab = cos.shape[0] // tm

    def kv_map(g, i):
        return (g, (i // tps) * tpt + jnp.maximum(i % tps - (tps - tpt), 0), 0)

    qkv_spec = pl.BlockSpec((None, tm, A_WIDTH), lambda g, i: (g, i, 0))
    gain_spec = pl.BlockSpec((None, 1, A_WIDTH), lambda g, i: (g, 0, 0))
    tab_spec = pl.BlockSpec((tm, LANES), lambda g, i: (i % ntab, 0))
    return pl.pallas_call(
        _a_proj_kernel,
        grid=(N_GROUPS, mt),
        in_specs=[
            pl.BlockSpec((tm, d), lambda g, i: (i, 0)),
            pl.BlockSpec((1, d), lambda g, i: (0, 0)),
            pl.BlockSpec((d, 3 * A_WIDTH), lambda g, i: (0, g)),
            gain_spec, gain_spec, tab_spec, tab_spec,
            pl.BlockSpec((MXU_DIM, MXU_DIM), lambda g, i: (0, 0)),
        ],
        out_specs=[qkv_spec, qkv_spec, qkv_spec,
                   pl.BlockSpec((None, tm, 2 * A_WIDTH), kv_map)],
        out_shape=[jax.ShapeDtypeStruct((N_GROUPS, m, A_WIDTH), BF16)] * 3
        + [jax.ShapeDtypeStruct((N_GROUPS, m // seq * tail, 2 * A_WIDTH), F32)],
        compiler_params=_cparams("arbitrary", "arbitrary"),
        name="a_proj",
    )(x, gmix, w, gq, gk, cos, sin, ones)


def _band_attn_kernel(q_ref, kp_ref, kc_ref, vp_ref, vc_ref, o_ref, lse_ref, kb, vb, *, nsub):
    i = pl.program_id(2)
    kb[0:A_BLOCK, :] = kp_ref[...]
    kb[A_BLOCK:, :] = kc_ref[...]
    vb[0:A_BLOCK, :] = vp_ref[...]
    vb[A_BLOCK:, :] = vc_ref[...]
    heads_per_half = MXU_DIM // A_HEAD_DIM
    stacked = (heads_per_half * A_BLOCK, 2 * A_BLOCK)
    qi = lax.broadcasted_iota(jnp.int32, stacked, 0) & (A_BLOCK - 1)
    kj = lax.broadcasted_iota(jnp.int32, stacked, 1)
    band = (kj >= qi) & (kj <= qi + A_BLOCK)
    band_first = band & ((kj >= A_BLOCK) | (i > 0))
    head_of_lane = lax.broadcasted_iota(jnp.int32, (1, MXU_DIM), 1) // A_HEAD_DIM

    for s in range(nsub):
        valid = band_first if s == 0 else band
        rows = slice(s * A_BLOCK, (s + 1) * A_BLOCK)
        keys = slice(s * A_BLOCK, (s + 2) * A_BLOCK)
        for half in range(A_WIDTH // MXU_DIM):
            lanes = slice(half * MXU_DIM, (half + 1) * MXU_DIM)
            qh = q_ref[rows, lanes]
            qs = jnp.concatenate(
                [jnp.where(head_of_lane == hd, qh, jnp.zeros_like(qh))
                 for hd in range(heads_per_half)], axis=0)
            sc = _dot_nt(qs, kb[keys, lanes])
            sc = jnp.where(valid, sc, NEG)
            m = jnp.max(sc, axis=1, keepdims=True)
            p = jnp.exp(sc - m)
            l = jnp.sum(p, axis=1, keepdims=True)
            r = _dot(p.astype(BF16), vb[keys, lanes]) / l
            lse = m + jnp.log(l)
            o_half = jnp.zeros((A_BLOCK, MXU_DIM), F32)
            lse_half = jnp.zeros((A_BLOCK, MXU_DIM), F32)
            for hd in range(heads_per_half):
                hrows = slice(hd * A_BLOCK, (hd + 1) * A_BLOCK)
                sel = head_of_lane == hd
                o_half = jnp.where(sel, r[hrows, :], o_half)
                lse_half = jnp.where(sel, lse[hrows, :], lse_half)
            o_ref[rows, lanes] = o_half.astype(BF16)
            lse_ref[rows, lanes] = lse_half


def _band_attn(q3, k3, v3, g, batch, seq):
    dil = DILATIONS[g]
    sub = seq // dil
    tq = min(4 * A_BLOCK, sub)
    nsub = tq // A_BLOCK
    nb = sub // tq
    shape = (N_GROUPS, batch, sub, dil * A_WIDTH)
    q3, k3, v3 = (t.reshape(shape) for t in (q3, k3, v3))
    cur = pl.BlockSpec((None, None, tq, A_WIDTH), lambda b, r, i: (g, b, i, r))
    prev = pl.BlockSpec((None, None, A_BLOCK, A_WIDTH),
                        lambda b, r, i: (g, b, jnp.maximum(i * nsub - 1, 0), r))
    out = pl.BlockSpec((None, tq, A_WIDTH), lambda b, r, i: (b, i, r))
    o, lse = pl.pallas_call(
        functools.partial(_band_attn_kernel, nsub=nsub),
        grid=(batch, dil, nb),
        in_specs=[cur, prev, cur, prev, cur],
        out_specs=[out, out],
        out_shape=[jax.ShapeDtypeStruct((batch, sub, dil * A_WIDTH), BF16),
                   jax.ShapeDtypeStruct((batch, sub, dil * A_WIDTH), F32)],
        scratch_shapes=[pltpu.VMEM((tq + A_BLOCK, A_WIDTH), BF16),
                        pltpu.VMEM((tq + A_BLOCK, A_WIDTH), BF16)],
        compiler_params=_cparams("arbitrary", "arbitrary", "arbitrary"),
        name=f"band_attn_g{g}",
    )(q3, k3, k3, v3, v3)
    return o.reshape(batch * seq, A_WIDTH), lse.reshape(batch * seq, A_WIDTH)


def _a_out_kernel(o0_ref, o1_ref, o2_ref, l0_ref, l1_ref, l2_ref, x_ref, w_ref, out_ref):
    l0, l1, l2 = l0_ref[...], l1_ref[...], l2_ref[...]
    mx = jnp.maximum(jnp.maximum(l0, l1), l2)
    w0, w1, w2 = jnp.exp(l0 - mx), jnp.exp(l1 - mx), jnp.exp(l2 - mx)
    num = (w0 * o0_ref[...].astype(F32) + w1 * o1_ref[...].astype(F32)
           + w2 * o2_ref[...].astype(F32))
    o = num / (w0 + w1 + w2)
    out_ref[...] = x_ref[...] + _dot(o.astype(BF16), w_ref[...])


def _a_out(os_, lses, x, w, *, tm):
    m, d = x.shape
    row = lambda i: (i, 0)
    grp = pl.BlockSpec((tm, A_WIDTH), row)
    return pl.pallas_call(
        _a_out_kernel,
        grid=(m // tm,),
        in_specs=[grp] * 6 + [pl.BlockSpec((tm, d), row), _resident((A_WIDTH, d), lambda i: (0, 0))],
        out_specs=pl.BlockSpec((tm, d), row),
        out_shape=jax.ShapeDtypeStruct((m, d), F32),
        compiler_params=_cparams("arbitrary"),
        name="a_out",
    )(*os_, *lses, x, w)


def _decode_attn_kernel(qt_ref, newt_ref, c_ref, out_ref, ot_ref, lset_ref, *, dil, n_rel):
    b = pl.program_id(0)
    lg = c_ref.shape[1]
    onehot = lax.broadcasted_iota(jnp.int32, (1, qt_ref.shape[1]), 1) == b
    qcol = jnp.sum(jnp.where(onehot, qt_ref[...], 0.0), axis=1, keepdims=True)
    newcol = jnp.sum(jnp.where(onehot, newt_ref[...], 0.0), axis=1, keepdims=True)
    pos = lax.broadcasted_iota(jnp.int32, (1, lg), 1)
    back = lg - pos
    valid = ((back & (dil - 1)) == 0) & (back <= n_rel * dil)
    last = pos == lg - 1

    @pl.when(b == 0)
    def _():
        ot_ref[...] = jnp.zeros_like(ot_ref)
        lset_ref[...] = jnp.zeros_like(lset_ref)

    def shifted(slab, new):
        return jnp.where(last, new, pltpu.roll(slab, lg - 1, 1))

    for hd in range(A_HEADS):
        krows = slice(hd * A_HEAD_DIM, (hd + 1) * A_HEAD_DIM)
        vrows = slice(A_WIDTH + hd * A_HEAD_DIM, A_WIDTH + (hd + 1) * A_HEAD_DIM)
        qh = qcol[krows, :]
        kt = c_ref[krows, :]
        sc = jnp.sum(kt * qh, axis=0, keepdims=True)
        sc_new = jnp.sum(newcol[krows, :] * qh, axis=0, keepdims=True)
        sc = jnp.where(valid, sc, NEG)
        m = jnp.maximum(jnp.max(sc, axis=1, keepdims=True), sc_new)
        p = jnp.where(valid, jnp.exp(sc - m), 0.0)
        p_new = jnp.exp(sc_new - m)
        l = jnp.sum(p, axis=1, keepdims=True) + p_new
        vt = c_ref[vrows, :]
        acc = jnp.sum(vt * p, axis=1, keepdims=True) + p_new * newcol[vrows, :]
        out_ref[krows, :] = shifted(kt, newcol[krows, :])
        out_ref[vrows, :] = shifted(vt, newcol[vrows, :])
        ot_ref[krows, :] = jnp.where(onehot, acc / l, ot_ref[krows, :])
        lset_ref[hd:hd + 1, :] = jnp.where(onehot, m + jnp.log(l), lset_ref[hd:hd + 1, :])


def _decode_attn(qt, newt, cache, layer, prev_out, g):
    nl, bd, rows, lg = cache.shape
    kernel = functools.partial(_decode_attn_kernel, dil=DILATIONS[g], n_rel=WINDOWS[g] // DILATIONS[g])
    in_specs = [
        pl.BlockSpec((A_WIDTH, bd), lambda b: (0, 0)),
        pl.BlockSpec((rows, bd), lambda b: (0, 0)),
        pl.BlockSpec((None, None, rows, lg), lambda b: (layer, b, 0, 0)),
    ]
    args = [qt, newt, cache]
    aliases = {}
    if prev_out is not None:
        in_specs.append(pl.BlockSpec(memory_space=pl.ANY))
        args.append(prev_out)
        aliases = {3: 0}
        kernel = functools.partial(_drop_arg, kernel, 3)
    return pl.pallas_call(
        kernel,
        grid=(bd,),
        in_specs=in_specs,
        out_specs=[pl.BlockSpec((None, None, rows, lg), lambda b: (layer, b, 0, 0)),
                   pl.BlockSpec((A_WIDTH, bd), lambda b: (0, 0)),
                   pl.BlockSpec((A_HEADS, bd), lambda b: (0, 0))],
        out_shape=[jax.ShapeDtypeStruct(cache.shape, F32),
                   jax.ShapeDtypeStruct((A_WIDTH, bd), F32),
                   jax.ShapeDtypeStruct((A_HEADS, bd), F32)],
        input_output_aliases=aliases,
        compiler_params=_cparams("arbitrary"),
        name=f"decode_attn_g{g}",
    )(*args)


def _drop_arg(kernel, idx, *refs):
    return kernel(*refs[:idx], *refs[idx + 1:])


def _ffn_kernel(x_ref, g_ref, wg_ref, wu_ref, wd_ref, out_ref, *, chunk):
    x = x_ref[...]
    h = _rms(x, g_ref[...]).astype(BF16)
    out_ref[...] = x
    for c in range(wg_ref.shape[1] // chunk):
        cols = slice(c * chunk, (c + 1) * chunk)
        gate = _dot(h, wg_ref[:, cols])
        up = _dot(h, wu_ref[:, cols])
        out_ref[...] += _dot((_silu(gate) * up).astype(BF16), wd_ref[cols, :])


def _ffn(x, g, wg, wu, wd, *, tm):
    m, d = x.shape
    f = wg.shape[1]
    row = lambda i: (i, 0)
    return pl.pallas_call(
        functools.partial(_ffn_kernel, chunk=MXU_DIM),
        grid=(m // tm,),
        in_specs=[pl.BlockSpec((tm, d), row), pl.BlockSpec((1, d), lambda i: (0, 0)),
                  _resident((d, f), lambda i: (0, 0)), _resident((d, f), lambda i: (0, 0)),
                  _resident((f, d), lambda i: (0, 0))],
        out_specs=pl.BlockSpec((tm, d), row),
        out_shape=jax.ShapeDtypeStruct((m, d), F32),
        compiler_params=_cparams("arbitrary"),
        name="ffn",
    )(x, g, wg, wu, wd)


def _r_proj_kernel(x_ref, gmix_ref, w_ref, cos_ref, sin_ref, out_ref, *, qk_dim):
    j = pl.program_id(0)
    h = _rms(x_ref[...], gmix_ref[...]).astype(BF16)
    p = _dot(h, w_ref[...])
    half = qk_dim // 2

    @pl.when(j == 0)
    def _():
        cos, sin = cos_ref[...], sin_ref[...]
        for hd in range(2 * R_HEADS):
            scale = 1.0 if hd < R_HEADS else qk_dim ** -0.5
            c0 = hd * qk_dim
            x1, x2 = p[:, c0:c0 + half], p[:, c0 + half:c0 + qk_dim]
            out_ref[:, c0:c0 + half] = ((x1 * cos - x2 * sin) * scale).astype(BF16)
            out_ref[:, c0 + half:c0 + qk_dim] = ((x2 * cos + x1 * sin) * scale).astype(BF16)

    @pl.when(j != 0)
    def _():
        out_ref[...] = p.astype(BF16)


def _r_proj(x, gmix, w, cos, sin, *, tm):
    m, d = x.shape
    n = w.shape[1]
    qk_dim = n // (6 * R_HEADS)
    tn = 2 * R_HEADS * qk_dim
    ntab = cos.shape[0] // tm
    tab = pl.BlockSpec((tm, qk_dim // 2), lambda j, i: (i % ntab, 0))
    return pl.pallas_call(
        functools.partial(_r_proj_kernel, qk_dim=qk_dim),
        grid=(n // tn, m // tm),
        in_specs=[pl.BlockSpec((tm, d), lambda j, i: (i, 0)),
                  pl.BlockSpec((1, d), lambda j, i: (0, 0)),
                  pl.BlockSpec((d, tn), lambda j, i: (0, j)), tab, tab],
        out_specs=pl.BlockSpec((tm, tn), lambda j, i: (i, j)),
        out_shape=jax.ShapeDtypeStruct((m, n), BF16),
        compiler_params=_cparams("arbitrary", "arbitrary"),
        name="r_proj",
    )(x, gmix, w, cos, sin)


def _retention_kernel(q_ref, k_ref, v_ref, gate_ref, dmask_ref, qdec_ref, kdec_ref, cdec_ref,
                      gout_ref, y_ref, sfin_ref, st):
    c = pl.program_id(2)

    @pl.when(c == 0)
    def _():
        st[...] = jnp.zeros_like(st)

    q, k, v = q_ref[...], k_ref[...], v_ref[...]
    s_prev = st[...]
    inter = _dot(q, s_prev.astype(BF16)) * qdec_ref[...]
    scores = _dot_nt(q, k) * dmask_ref[...]
    o = inter + _dot(scores.astype(BF16), v)
    kd = (k.astype(F32) * kdec_ref[...]).astype(BF16)
    st[...] = s_prev * cdec_ref[:, 0:1] + _dot_tn(kd, v)
    y = _rms(o, gout_ref[...]) * _silu(gate_ref[...].astype(F32))
    y_ref[...] = y.astype(BF16)

    @pl.when(c == pl.num_programs(2) - 1)
    def _():
        sfin_ref[...] = st[...]


def _retention_tables(chunk):
    lg = jnp.log1p(-jnp.exp2(-5.0 - jnp.arange(R_HEADS, dtype=F32)))
    idx = jnp.arange(chunk, dtype=F32)
    rel = idx[:, None] - idx[None, :]
    dmask = jnp.where(rel >= 0, jnp.exp(jnp.maximum(rel, 0.0) * lg[:, None, None]), 0.0)
    qdec = jnp.exp((idx + 1.0)[None, :] * lg[:, None])[:, :, None]
    kdec = jnp.exp((chunk - 1.0 - idx)[None, :] * lg[:, None])[:, :, None]
    cdec = jnp.broadcast_to(jnp.exp(chunk * lg)[:, None, None], (R_HEADS, 1, LANES))
    return dmask, qdec, kdec, cdec


def _retention(p, gout, batch, seq, *, chunk):
    m, n = p.shape
    qk = n // (6 * R_HEADS)
    dv = 2 * qk
    nc = seq // chunk
    dmask, qdec, kdec, cdec = _retention_tables(chunk)
    row = lambda b, h, c: b * nc + c
    per_head = lambda shape: pl.BlockSpec((None,) + shape, lambda b, h, c: (h, 0, 0))
    return pl.pallas_call(
        _retention_kernel,
        grid=(batch, R_HEADS, nc),
        in_specs=[
            pl.BlockSpec((chunk, qk), lambda b, h, c: (row(b, h, c), h)),
            pl.BlockSpec((chunk, qk), lambda b, h, c: (row(b, h, c), R_HEADS + h)),
            pl.BlockSpec((chunk, dv), lambda b, h, c: (row(b, h, c), R_HEADS + h)),
            pl.BlockSpec((chunk, dv), lambda b, h, c: (row(b, h, c), 2 * R_HEADS + h)),
            per_head((chunk, chunk)), per_head((chunk, 1)), per_head((chunk, 1)),
            per_head((1, LANES)), per_head((1, dv)),
        ],
        out_specs=[pl.BlockSpec((chunk, dv), lambda b, h, c: (row(b, h, c), h)),
                   pl.BlockSpec((None, None, qk, dv), lambda b, h, c: (b, h, 0, 0))],
        out_shape=[jax.ShapeDtypeStruct((m, R_HEADS * dv), BF16),
                   jax.ShapeDtypeStruct((batch, R_HEADS, qk, dv), F32)],
        scratch_shapes=[pltpu.VMEM((qk, dv), F32)],
        compiler_params=_cparams("arbitrary", "arbitrary", "arbitrary"),
        name="retention",
    )(p, p, p, p, dmask, qdec, kdec, cdec, gout)


def _retention_step_kernel(p_ref, s_ref, gout_ref, y_ref, snew_ref, *, qk, dv):
    eye = (lax.broadcasted_iota(jnp.int32, (qk, qk), 0)
           == lax.broadcasted_iota(jnp.int32, (qk, qk), 1))
    for hd in range(R_HEADS):
        gamma = 1.0 - 2.0 ** (-5 - hd)
        q = p_ref[:, hd * qk:(hd + 1) * qk].astype(F32)
        k = p_ref[:, (R_HEADS + hd) * qk:(R_HEADS + hd + 1) * qk].astype(F32)
        v = p_ref[:, 2 * R_HEADS * qk + hd * dv:2 * R_HEADS * qk + (hd + 1) * dv].astype(F32)
        gate = p_ref[:, 2 * R_HEADS * qk + (R_HEADS + hd) * dv:
                     2 * R_HEADS * qk + (R_HEADS + hd + 1) * dv].astype(F32)
        qcol = jnp.sum(jnp.where(eye, q, 0.0), axis=1, keepdims=True)
        kcol = jnp.sum(jnp.where(eye, k, 0.0), axis=1, keepdims=True)
        s_prev = s_ref[hd]
        inter = jnp.sum(s_prev * qcol, axis=0, keepdims=True) * gamma
        o = inter + jnp.sum(q * k, axis=1, keepdims=True) * v
        snew_ref[hd] = s_prev * gamma + kcol * v
        y = _rms(o, gout_ref[hd:hd + 1, :]) * _silu(gate)
        y_ref[:, hd * dv:(hd + 1) * dv] = y.astype(BF16)


def _retention_step(p, state, layer, prev_out, gout):
    bd = p.shape[0]
    nl, _, nh, qk, dv = state.shape
    kernel = functools.partial(_retention_step_kernel, qk=qk, dv=dv)
    in_specs = [pl.BlockSpec((None, 1, p.shape[2]), lambda b: (b, 0, 0)),
                pl.BlockSpec((None, None, nh, qk, dv), lambda b: (layer, b, 0, 0, 0)),
                pl.BlockSpec((nh, dv), lambda b: (0, 0))]
    args = [p, state, gout]
    aliases = {}
    if prev_out is not None:
        in_specs.append(pl.BlockSpec(memory_space=pl.ANY))
        args.append(prev_out)
        aliases = {3: 1}
        kernel = functools.partial(_drop_arg, kernel, 3)
    return pl.pallas_call(
        kernel,
        grid=(bd,),
        in_specs=in_specs,
        out_specs=[pl.BlockSpec((None, 1, nh * dv), lambda b: (b, 0, 0)),
                   pl.BlockSpec((None, None, nh, qk, dv), lambda b: (layer, b, 0, 0, 0))],
        out_shape=[jax.ShapeDtypeStruct((bd, 1, nh * dv), BF16),
                   jax.ShapeDtypeStruct(state.shape, F32)],
        input_output_aliases=aliases,
        compiler_params=_cparams("arbitrary"),
        name="retention_step",
    )(*args)


def _proj_res_kernel(y_ref, x_ref, w_ref, out_ref):
    out_ref[...] = x_ref[...] + _dot(y_ref[...], w_ref[...])


def _proj_res(y, x, w, *, tm):
    m, d = x.shape
    kdim = y.shape[1]
    row = lambda i: (i, 0)
    return pl.pallas_call(
        _proj_res_kernel,
        grid=(m // tm,),
        in_specs=[pl.BlockSpec((tm, kdim), row), pl.BlockSpec((tm, d), row),
                  _resident((kdim, d), lambda i: (0, 0))],
        out_specs=pl.BlockSpec((tm, d), row),
        out_shape=jax.ShapeDtypeStruct((m, d), F32),
        compiler_params=_cparams("arbitrary"),
        name="proj_res",
    )(y, x, w)


def _rope_tables(pos, half, width, sign_fold):
    lane = jnp.arange(width)
    inv = ROPE_THETA ** (-(lane % half).astype(F32) / half)
    ang = pos.astype(F32)[:, None] * inv[None, :]
    cos, sin = jnp.cos(ang), jnp.sin(ang)
    if sign_fold:
        sin = jnp.where((lane % (2 * half)) < half, -sin, sin)
    return cos, sin


def kernel(x_prompt, x_sample, cache_kv_g0, cache_kv_g1, cache_kv_g2, state_retention,
           mix_norm, ffn_norm, a_w_qkv, a_q_norm, a_k_norm, a_w_o,
           r_w_in, r_out_norm, r_w_o, f_w_gate, f_w_up, f_w_down):
    batch, seq, d = x_prompt.shape
    bd, dec_seq, _ = x_sample.shape
    assert dec_seq == 1
    depth = mix_norm.shape[0]
    caches = (cache_kv_g0, cache_kv_g1, cache_kv_g2)
    past = PAST_LEN
    tail = max(min(w, seq) for w in WINDOWS)
    tm = 512
    qk_dim = r_w_in.shape[2] // (6 * R_HEADS)

    xp = x_prompt.reshape(batch * seq, d)
    xs = x_sample.reshape(bd, d)

    caches_t = [jnp.transpose(c, (0, 1, 3, 4, 5, 2)).reshape(c.shape[0], bd, 2 * A_WIDTH, c.shape[2])
                for c in caches]

    pos_p = jnp.arange(seq)
    pos_s = jnp.full((bd,), past)
    a_tabs_p = _rope_tables(pos_p, A_HEAD_DIM // 2, LANES, True)
    a_tabs_s = _rope_tables(pos_s, A_HEAD_DIM // 2, LANES, True)
    r_tabs_p = _rope_tables(pos_p, qk_dim // 2, qk_dim // 2, False)
    r_tabs_s = _rope_tables(pos_s, qk_dim // 2, qk_dim // 2, False)
    lane = jnp.arange(MXU_DIM)
    ones_bd = (lane[:, None] // A_HEAD_DIM == lane[None, :] // A_HEAD_DIM).astype(BF16)

    kv_p = [[] for _ in range(N_GROUPS)]
    kv_s = [None] * N_GROUPS
    ret_p, ret_s = [], None
    for i in range(depth):
        j = i // 2
        gmix = mix_norm[i][None, :]
        if i % 2 == 0:
            w_qkv = a_w_qkv[j].astype(BF16)
            w_o = a_w_o[j].astype(BF16)
            gq = jnp.tile(a_q_norm[j], (1, A_HEADS))[:, None, :]
            gk = jnp.tile(a_k_norm[j], (1, A_HEADS))[:, None, :]
            q3, k3, v3, kvt = _a_proj(xp, gmix, w_qkv, gq, gk, *a_tabs_p, ones_bd,
                                      tm=tm, seq=seq, tail=tail)
            os_, lses = [], []
            for g in range(N_GROUPS):
                o, lse = _band_attn(q3, k3, v3, g, batch, seq)
                os_.append(o)
                lses.append(lse)
                keep = min(WINDOWS[g], seq)
                rows = kvt[g].reshape(batch, tail, 2, A_HEADS, A_HEAD_DIM)[:, tail - keep:]
                kv_p[g].append(rows)
            xp = _a_out(os_, lses, xp, w_o, tm=tm)
            q3s, _, _, kvs = _a_proj(xs, gmix, w_qkv, gq, gk, *a_tabs_s, ones_bd,
                                     tm=bd, seq=bd, tail=bd)
            os_, lses = [], []
            for g in range(N_GROUPS):
                qt = q3s[g].astype(F32).T
                newt = kvs[g].T
                kv_s[g], ot, lset = _decode_attn(qt, newt, caches_t[g], j, kv_s[g], g)
                os_.append(ot.T)
                lses.append(jnp.repeat(lset.T, A_HEAD_DIM, axis=1))
            xs = _a_out(os_, lses, xs, w_o, tm=bd)
        else:
            w_in = r_w_in[j].astype(BF16)
            w_o = r_w_o[j].astype(BF16)
            gout = r_out_norm[j]
            p = _r_proj(xp, gmix, w_in, *r_tabs_p, tm=tm)
            y, s_fin = _retention(p, gout[:, None, :], batch, seq, chunk=R_CHUNK)
            ret_p.append(s_fin)
            xp = _proj_res(y, xp, w_o, tm=tm)
            ps = _r_proj(xs, gmix, w_in, *r_tabs_s, tm=bd)
            ys, ret_s = _retention_step(ps[:, None, :], state_retention, j, ret_s, gout)
            xs = _proj_res(ys.reshape(bd, -1), xs, w_o, tm=bd)
        gffn = ffn_norm[i][None, :]
        wg, wu, wd = (w[i].astype(BF16) for w in (f_w_gate, f_w_up, f_w_down))
        xp = _ffn(xp, gffn, wg, wu, wd, tm=tm)
        xs = _ffn(xs, gffn, wg, wu, wd, tm=bd)

    def buffers(t):
        nl, _, _, slots = t.shape
        return jnp.transpose(t.reshape(nl, bd, 2, A_HEADS, A_HEAD_DIM, slots), (0, 1, 5, 2, 3, 4))

    return (xp.reshape(batch, seq, d), xs.reshape(bd, 1, d),
            jnp.stack(kv_p[0]), buffers(kv_s[0]),
            jnp.stack(kv_p[1]), buffers(kv_s[1]),
            jnp.stack(kv_p[2]), buffers(kv_s[2]),
            jnp.stack(ret_p), ret_s)
```

```python
import functools

import jax
import jax.numpy as jnp
from jax import lax
from jax.experimental import pallas as pl
from jax.experimental.pallas import tpu as pltpu

F32 = jnp.float32
BF16 = jnp.bfloat16

EPS = 1e-6
ROPE_THETA = 10000.0
WINDOWS = (128, 512, 2048)
DILATIONS = (1, 4, 16)
N_GROUPS = 3
A_HEADS = 8
A_HEAD_DIM = 64
A_WIDTH = A_HEADS * A_HEAD_DIM
A_BLOCK = 128
R_HEADS = 4
R_CHUNK = 256
PAST_LEN = 2048
NEG = -1e30

VMEM_LIMIT_BYTES = 56 * 1024 * 1024
LANES = 128
MXU_DIM = 256
ROW_TILE = 512
DECODE_SEQS_PER_STEP = (4, 4, 1)


def _cparams(*sem):
    return pltpu.CompilerParams(dimension_semantics=sem, vmem_limit_bytes=VMEM_LIMIT_BYTES)


def _resident(shape, index_map):
    return pl.BlockSpec(shape, index_map, pipeline_mode=pl.Buffered(1))


def _rms(x, g):
    ms = jnp.mean(x * x, axis=-1, keepdims=True)
    return x * lax.rsqrt(ms + EPS) * g


def _silu(x):
    return x * (1.0 / (1.0 + jnp.exp(-x)))


def _dot(a, b):
    return jnp.dot(a, b, preferred_element_type=F32)


def _dot_nt(a, b):
    return lax.dot_general(a, b, (((1,), (1,)), ((), ())), preferred_element_type=F32)


def _dot_tn(a, b):
    return lax.dot_general(a, b, (((0,), (0,)), ((), ())), preferred_element_type=F32)


def _a_proj_kernel(x_ref, gmix_ref, w_ref, gq_ref, gk_ref, cos_ref, sin_ref, ones_ref, *refs,
                   dils, tail_blocks, tiles_per_seq, transposed_q):
    n_qkv = N_GROUPS if transposed_q else 3 * N_GROUPS
    qkv_refs, kvt_refs, deint = refs[:n_qkv], refs[n_qkv:n_qkv + N_GROUPS], refs[n_qkv + N_GROUPS]
    tm = x_ref.shape[0]
    tile_in_seq = pl.program_id(0) % tiles_per_seq
    h = _rms(x_ref[...], gmix_ref[...]).astype(BF16)
    cos = jnp.concatenate([cos_ref[...]] * (A_WIDTH // LANES), axis=1)
    sin = jnp.concatenate([sin_ref[...]] * (A_WIDTH // LANES), axis=1)
    lane = lax.broadcasted_iota(jnp.int32, (1, A_WIDTH), 1)
    first_half = (lane % A_HEAD_DIM) < (A_HEAD_DIM // 2)
    ones = ones_ref[...]

    def normed_rope(p, gain):
        y = (p * p).astype(BF16)
        ss = jnp.concatenate([_dot(y[:, :MXU_DIM], ones), _dot(y[:, MXU_DIM:], ones)], axis=1)
        pn = p * lax.rsqrt(ss * (1.0 / A_HEAD_DIM) + EPS) * gain
        rot = jnp.where(first_half,
                        pltpu.roll(pn, A_WIDTH - A_HEAD_DIM // 2, 1),
                        pltpu.roll(pn, A_HEAD_DIM // 2, 1))
        return pn * cos + rot * sin

    def store_dilated(out_ref, slot, val, dil):
        if dil == 1:
            out_ref[...] = val.astype(BF16)
            return
        tiles = A_WIDTH // LANES
        for c in range(tiles):
            deint[slot * tiles + c] = val[:, c * LANES:(c + 1) * LANES]
        for r in range(dil):
            for c in range(tiles):
                out_ref[:, r * A_WIDTH + c * LANES:r * A_WIDTH + (c + 1) * LANES] = (
                    deint[slot * tiles + c, pl.ds(r, tm // dil, stride=dil), :].astype(BF16))

    for g in range(N_GROUPS):
        c0 = 3 * g * A_WIDTH
        q = normed_rope(_dot(h, w_ref[:, c0:c0 + A_WIDTH]), gq_ref[g]) * (A_HEAD_DIM ** -0.5)
        k = normed_rope(_dot(h, w_ref[:, c0 + A_WIDTH:c0 + 2 * A_WIDTH]), gk_ref[g])
        v = _dot(h, w_ref[:, c0 + 2 * A_WIDTH:c0 + 3 * A_WIDTH])
        if transposed_q:
            qkv_refs[g][...] = q.T
        else:
            store_dilated(qkv_refs[3 * g], 0, q, dils[g])
            store_dilated(qkv_refs[3 * g + 1], 1, k, dils[g])
            store_dilated(qkv_refs[3 * g + 2], 2, v, dils[g])

        kvt_ref = kvt_refs[g]
        tw = kvt_ref.shape[1]

        @pl.when(tile_in_seq >= tiles_per_seq - tail_blocks[g])
        def _(k=k, v=v, kvt_ref=kvt_ref, tw=tw):
            kvt_ref[0:A_WIDTH, :] = k[tm - tw:, :].T
            kvt_ref[A_WIDTH:, :] = v[tm - tw:, :].T


def _a_proj(x, gmix, w_all, layer, gq, gk, cos, sin, ones, *, tm, seq, keeps, dils, transposed_q):
    m, d = x.shape
    n_seq = m // seq
    tps = seq // tm
    ntab = cos.shape[0] // tm
    tws = [min(tm, kp) for kp in keeps]
    tail_blocks = tuple(kp // tw for kp, tw in zip(keeps, tws))

    def kvt_map(nb):
        return lambda i: (i // tps, 0, jnp.maximum(i % tps - (tps - nb), 0))

    if transposed_q:
        qkv_specs = [pl.BlockSpec((A_WIDTH, tm), lambda i: (0, i))] * N_GROUPS
        qkv_shapes = [jax.ShapeDtypeStruct((A_WIDTH, m), F32)] * N_GROUPS
    else:
        qkv_specs, qkv_shapes = [], []
        for dil in dils:
            qkv_specs += [pl.BlockSpec((tm // dil, dil * A_WIDTH), lambda i: (i, 0))] * 3
            qkv_shapes += [jax.ShapeDtypeStruct((m // dil, dil * A_WIDTH), BF16)] * 3
    kvt_specs = [pl.BlockSpec((None, 2 * A_WIDTH, tw), kvt_map(nb)) for tw, nb in zip(tws, tail_blocks)]
    kvt_shapes = [jax.ShapeDtypeStruct((n_seq, 2 * A_WIDTH, kp), F32) for kp in keeps]
    gain_spec = pl.BlockSpec((N_GROUPS, 1, A_WIDTH), lambda i: (0, 0, 0))
    tab_spec = pl.BlockSpec((tm, LANES), lambda i: (i % ntab, 0))
    outs = pl.pallas_call(
        functools.partial(_a_proj_kernel, dils=dils, tail_blocks=tail_blocks, tiles_per_seq=tps,
                          transposed_q=transposed_q),
        grid=(m // tm,),
        in_specs=[
            pl.BlockSpec((tm, d), lambda i: (i, 0)),
            pl.BlockSpec((1, d), lambda i: (0, 0)),
            _resident((None, d, 3 * N_GROUPS * A_WIDTH), lambda i: (layer, 0, 0)),
            gain_spec, gain_spec, tab_spec, tab_spec,
            pl.BlockSpec((MXU_DIM, MXU_DIM), lambda i: (0, 0)),
        ],
        out_specs=qkv_specs + kvt_specs,
        out_shape=qkv_shapes + kvt_shapes,
        scratch_shapes=[pltpu.VMEM((3 * A_WIDTH // LANES, tm, LANES), F32)],
        compiler_params=_cparams("arbitrary"),
        name="a_proj",
    )(x, gmix, w_all, gq, gk, cos, sin, ones)
    return outs[:-N_GROUPS], outs[-N_GROUPS:]


def _band_attn_kernel(q_ref, kp_ref, kc_ref, vp_ref, vc_ref, o_ref, lse_ref, kb, vb, *, nsub):
    i = pl.program_id(2)
    kb[0:A_BLOCK, :] = kp_ref[...]
    kb[A_BLOCK:, :] = kc_ref[...]
    vb[0:A_BLOCK, :] = vp_ref[...]
    vb[A_BLOCK:, :] = vc_ref[...]
    heads_per_half = MXU_DIM // A_HEAD_DIM
    stacked = (heads_per_half * A_BLOCK, 2 * A_BLOCK)
    qi = lax.broadcasted_iota(jnp.int32, stacked, 0) & (A_BLOCK - 1)
    kj = lax.broadcasted_iota(jnp.int32, stacked, 1)
    band = (kj >= qi) & (kj <= qi + A_BLOCK)
    band_first = band & ((kj >= A_BLOCK) | (i > 0))
    head_of_lane = lax.broadcasted_iota(jnp.int32, (1, MXU_DIM), 1) // A_HEAD_DIM

    for s in range(nsub):
        valid = band_first if s == 0 else band
        rows = slice(s * A_BLOCK, (s + 1) * A_BLOCK)
        keys = slice(s * A_BLOCK, (s + 2) * A_BLOCK)
        for half in range(A_WIDTH // MXU_DIM):
            lanes = slice(half * MXU_DIM, (half + 1) * MXU_DIM)
            qh = q_ref[rows, lanes]
            qs = jnp.concatenate(
                [jnp.where(head_of_lane == hd, qh, jnp.zeros_like(qh))
                 for hd in range(heads_per_half)], axis=0)
            sc = _dot_nt(qs, kb[keys, lanes])
            sc = jnp.where(valid, sc, NEG)
            m = jnp.max(sc, axis=1, keepdims=True)
            p = jnp.exp(sc - m)
            l = jnp.sum(p, axis=1, keepdims=True)
            r = _dot(p.astype(BF16), vb[keys, lanes]) * (1.0 / l)
            lse = m + jnp.log(l)
            o_half = jnp.zeros((A_BLOCK, MXU_DIM), F32)
            lse_half = jnp.zeros((A_BLOCK, MXU_DIM), F32)
            for hd in range(heads_per_half):
                hrows = slice(hd * A_BLOCK, (hd + 1) * A_BLOCK)
                sel = head_of_lane == hd
                o_half = jnp.where(sel, r[hrows, :], o_half)
                lse_half = jnp.where(sel, lse[hrows, :], lse_half)
            o_ref[rows, lanes] = o_half.astype(BF16)
            lse_ref[rows, lanes] = lse_half


def _band_attn(q, k, v, g, batch, seq):
    dil = DILATIONS[g]
    sub = seq // dil
    tq = min(4 * A_BLOCK, sub)
    nsub = tq // A_BLOCK
    nb = sub // tq
    cur = pl.BlockSpec((tq, A_WIDTH), lambda b, r, i: (b * nb + i, r))
    prev = pl.BlockSpec((A_BLOCK, A_WIDTH),
                        lambda b, r, i: (b * nb * nsub + jnp.maximum(i * nsub - 1, 0), r))
    return pl.pallas_call(
        functools.partial(_band_attn_kernel, nsub=nsub),
        grid=(batch, dil, nb),
        in_specs=[cur, prev, cur, prev, cur],
        out_specs=[cur, cur],
        out_shape=[jax.ShapeDtypeStruct(q.shape, BF16), jax.ShapeDtypeStruct(q.shape, F32)],
        scratch_shapes=[pltpu.VMEM((tq + A_BLOCK, A_WIDTH), BF16),
                        pltpu.VMEM((tq + A_BLOCK, A_WIDTH), BF16)],
        compiler_params=_cparams("arbitrary", "arbitrary", "arbitrary"),
        name=f"band_attn_g{g}",
    )(q, k, k, v, v)


def _a_out_kernel(o0_ref, o1_ref, o2_ref, l0_ref, l1_ref, l2_ref, x_ref, w_ref, out_ref, scr, *, dils):
    tm = x_ref.shape[0]

    def natural(ref, slot, dil):
        if dil == 1:
            return ref[...].astype(F32)
        tiles = A_WIDTH // LANES
        for r in range(dil):
            for c in range(tiles):
                scr[slot * tiles + c, pl.ds(r, tm // dil, stride=dil), :] = (
                    ref[:, r * A_WIDTH + c * LANES:r * A_WIDTH + (c + 1) * LANES].astype(F32))
        return jnp.concatenate([scr[slot * tiles + c] for c in range(tiles)], axis=1)

    o0, o1, o2 = (natural(r, s, d) for s, (r, d) in enumerate(zip((o0_ref, o1_ref, o2_ref), dils)))
    l0, l1, l2 = (natural(r, 3 + s, d) for s, (r, d) in enumerate(zip((l0_ref, l1_ref, l2_ref), dils)))
    mx = jnp.maximum(jnp.maximum(l0, l1), l2)
    w0, w1, w2 = jnp.exp(l0 - mx), jnp.exp(l1 - mx), jnp.exp(l2 - mx)
    o = (w0 * o0 + w1 * o1 + w2 * o2) / (w0 + w1 + w2)
    out_ref[...] = x_ref[...] + _dot(o.astype(BF16), w_ref[...])


def _a_out(os_, lses, x, w_all, layer, *, tm, dils):
    m, d = x.shape
    row = lambda i: (i, 0)
    grp = [pl.BlockSpec((tm // dil, dil * A_WIDTH), row) for dil in dils]
    return pl.pallas_call(
        functools.partial(_a_out_kernel, dils=dils),
        grid=(m // tm,),
        in_specs=grp + grp + [pl.BlockSpec((tm, d), row),
                              _resident((None, A_WIDTH, d), lambda i: (layer, 0, 0))],
        out_specs=pl.BlockSpec((tm, d), row),
        out_shape=jax.ShapeDtypeStruct((m, d), F32),
        scratch_shapes=[pltpu.VMEM((2 * N_GROUPS * A_WIDTH // LANES, tm, LANES), F32)],
        compiler_params=_cparams("arbitrary"),
        name="a_out",
    )(*os_, *lses, x, w_all)


def _decode_attn_kernel(qt_ref, newt_ref, c_ref, out_ref, ot_ref, lset_ref, *, dil, n_rel):
    step = pl.program_id(0)
    per_step, _, lg = c_ref.shape
    lane_seq = lax.broadcasted_iota(jnp.int32, (1, qt_ref.shape[1]), 1)
    pos = lax.broadcasted_iota(jnp.int32, (1, lg), 1)
    back = lg - pos
    valid = ((back & (dil - 1)) == 0) & (back <= n_rel * dil)
    last = pos == lg - 1

    @pl.when(step == 0)
    def _():
        ot_ref[...] = jnp.zeros_like(ot_ref)
        lset_ref[...] = jnp.zeros_like(lset_ref)

    def shifted(slab, new):
        return jnp.where(last, new, pltpu.roll(slab, lg - 1, 1))

    for e in range(per_step):
        onehot = lane_seq == step * per_step + e
        qcol = jnp.sum(jnp.where(onehot, qt_ref[...], 0.0), axis=1, keepdims=True)
        newcol = jnp.sum(jnp.where(onehot, newt_ref[...], 0.0), axis=1, keepdims=True)
        for hd in range(A_HEADS):
            krows = slice(hd * A_HEAD_DIM, (hd + 1) * A_HEAD_DIM)
            vrows = slice(A_WIDTH + hd * A_HEAD_DIM, A_WIDTH + (hd + 1) * A_HEAD_DIM)
            qh = qcol[krows, :]
            kt = c_ref[e, krows, :]
            sc = jnp.sum(kt * qh, axis=0, keepdims=True)
            sc_new = jnp.sum(newcol[krows, :] * qh, axis=0, keepdims=True)
            sc = jnp.where(valid, sc, NEG)
            m = jnp.maximum(jnp.max(sc, axis=1, keepdims=True), sc_new)
            p = jnp.where(valid, jnp.exp(sc - m), 0.0)
            p_new = jnp.exp(sc_new - m)
            l = jnp.sum(p, axis=1, keepdims=True) + p_new
            vt = c_ref[e, vrows, :]
            acc = jnp.sum(vt * p, axis=1, keepdims=True) + p_new * newcol[vrows, :]
            out_ref[e, krows, :] = shifted(kt, newcol[krows, :])
            out_ref[e, vrows, :] = shifted(vt, newcol[vrows, :])
            ot_ref[krows, :] = jnp.where(onehot, acc / l, ot_ref[krows, :])
            lset_ref[hd:hd + 1, :] = jnp.where(onehot, m + jnp.log(l), lset_ref[hd:hd + 1, :])


def _decode_attn(qt, newt, cache, layer, prev_out, g):
    nl, bd, rows, lg = cache.shape
    per_step = DECODE_SEQS_PER_STEP[g]
    kernel = functools.partial(_decode_attn_kernel, dil=DILATIONS[g], n_rel=WINDOWS[g] // DILATIONS[g])
    buf_spec = pl.BlockSpec((None, per_step, rows, lg), lambda b: (layer, b, 0, 0))
    in_specs = [
        pl.BlockSpec((A_WIDTH, bd), lambda b: (0, 0)),
        pl.BlockSpec((rows, bd), lambda b: (0, 0)),
        buf_spec,
    ]
    args = [qt, newt, cache]
    aliases = {}
    if prev_out is not None:
        in_specs.append(pl.BlockSpec(memory_space=pl.ANY))
        args.append(prev_out)
        aliases = {3: 0}
        kernel = functools.partial(_drop_arg, kernel, 3)
    return pl.pallas_call(
        kernel,
        grid=(bd // per_step,),
        in_specs=in_specs,
        out_specs=[buf_spec,
                   pl.BlockSpec((A_WIDTH, bd), lambda b: (0, 0)),
                   pl.BlockSpec((A_HEADS, bd), lambda b: (0, 0))],
        out_shape=[jax.ShapeDtypeStruct(cache.shape, F32),
                   jax.ShapeDtypeStruct((A_WIDTH, bd), F32),
                   jax.ShapeDtypeStruct((A_HEADS, bd), F32)],
        input_output_aliases=aliases,
        compiler_params=_cparams("arbitrary"),
        name=f"decode_attn_g{g}",
    )(*args)


def _drop_arg(kernel, idx, *refs):
    return kernel(*refs[:idx], *refs[idx + 1:])


def _ffn_kernel(x_ref, g_ref, wg_ref, wu_ref, wd_ref, out_ref, *, chunk):
    x = x_ref[...]
    h = _rms(x, g_ref[...]).astype(BF16)
    out_ref[...] = x
    for c in range(wg_ref.shape[1] // chunk):
        cols = slice(c * chunk, (c + 1) * chunk)
        gate = _dot(h, wg_ref[:, cols])
        up = _dot(h, wu_ref[:, cols])
        out_ref[...] += _dot((_silu(gate) * up).astype(BF16), wd_ref[cols, :])


def _ffn(x, g, wg_all, wu_all, wd_all, layer, *, tm):
    m, d = x.shape
    f = wg_all.shape[2]
    row = lambda i: (i, 0)
    sel = lambda i: (layer, 0, 0)
    return pl.pallas_call(
        functools.partial(_ffn_kernel, chunk=MXU_DIM),
        grid=(m // tm,),
        in_specs=[pl.BlockSpec((tm, d), row), pl.BlockSpec((1, d), lambda i: (0, 0)),
                  _resident((None, d, f), sel), _resident((None, d, f), sel),
                  _resident((None, f, d), sel)],
        out_specs=pl.BlockSpec((tm, d), row),
        out_shape=jax.ShapeDtypeStruct((m, d), F32),
        compiler_params=_cparams("arbitrary"),
        name="ffn",
    )(x, g, wg_all, wu_all, wd_all)


def _r_proj_kernel(x_ref, gmix_ref, w_ref, cos_ref, sin_ref, out_ref, *, qk_dim):
    h = _rms(x_ref[...], gmix_ref[...]).astype(BF16)
    half = qk_dim // 2
    nqk = 2 * R_HEADS * qk_dim
    cos, sin = cos_ref[...], sin_ref[...]
    p = _dot(h, w_ref[:, 0:nqk])
    for hd in range(2 * R_HEADS):
        scale = 1.0 if hd < R_HEADS else qk_dim ** -0.5
        c0 = hd * qk_dim
        x1, x2 = p[:, c0:c0 + half], p[:, c0 + half:c0 + qk_dim]
        out_ref[:, c0:c0 + half] = ((x1 * cos - x2 * sin) * scale).astype(BF16)
        out_ref[:, c0 + half:c0 + qk_dim] = ((x2 * cos + x1 * sin) * scale).astype(BF16)
    for c0 in range(nqk, w_ref.shape[1], nqk):
        out_ref[:, c0:c0 + nqk] = _dot(h, w_ref[:, c0:c0 + nqk]).astype(BF16)


def _r_proj(x, gmix, w_all, layer, cos, sin, *, tm):
    m, d = x.shape
    n = w_all.shape[2]
    qk_dim = n // (6 * R_HEADS)
    ntab = cos.shape[0] // tm
    tab = pl.BlockSpec((tm, qk_dim // 2), lambda i: (i % ntab, 0))
    return pl.pallas_call(
        functools.partial(_r_proj_kernel, qk_dim=qk_dim),
        grid=(m // tm,),
        in_specs=[pl.BlockSpec((tm, d), lambda i: (i, 0)),
                  pl.BlockSpec((1, d), lambda i: (0, 0)),
                  _resident((None, d, n), lambda i: (layer, 0, 0)), tab, tab],
        out_specs=pl.BlockSpec((tm, n), lambda i: (i, 0)),
        out_shape=jax.ShapeDtypeStruct((m, n), BF16),
        compiler_params=_cparams("arbitrary"),
        name="r_proj",
    )(x, gmix, w_all, cos, sin)


def _retention_kernel(p_ref, x_ref, dmask_ref, qdec_ref, kdec_ref, cdec_ref, gout_ref, wo_ref,
                      out_ref, state_ref, y_scr, *, qk, dv):
    c = pl.program_id(0)

    @pl.when(c == 0)
    def _():
        state_ref[...] = jnp.zeros_like(state_ref)

    v0 = 2 * R_HEADS * qk
    g0 = v0 + R_HEADS * dv
    for b in range(p_ref.shape[0]):
        for hd in range(R_HEADS):
            q = p_ref[b, :, hd * qk:(hd + 1) * qk]
            k = p_ref[b, :, (R_HEADS + hd) * qk:(R_HEADS + hd + 1) * qk]
            v = p_ref[b, :, v0 + hd * dv:v0 + (hd + 1) * dv]
            gate = p_ref[b, :, g0 + hd * dv:g0 + (hd + 1) * dv].astype(F32)
            s_prev = state_ref[b, hd]
            inter = _dot(q, s_prev.astype(BF16)) * qdec_ref[hd]
            scores = _dot_nt(q, k) * dmask_ref[hd]
            o = inter + _dot(scores.astype(BF16), v)
            kd = (k.astype(F32) * kdec_ref[hd]).astype(BF16)
            state_ref[b, hd] = s_prev * cdec_ref[hd][:, 0:1] + _dot_tn(kd, v)
            y = _rms(o, gout_ref[hd]) * _silu(gate)
            y_scr[b, :, hd * dv:(hd + 1) * dv] = y.astype(BF16)
        out_ref[b] = x_ref[b] + _dot(y_scr[b], wo_ref[...])


def _retention_tables(chunk):
    lg = jnp.log1p(-jnp.exp2(-5.0 - jnp.arange(R_HEADS, dtype=F32)))
    idx = jnp.arange(chunk, dtype=F32)
    rel = idx[:, None] - idx[None, :]
    dmask = jnp.where(rel >= 0, jnp.exp(jnp.maximum(rel, 0.0) * lg[:, None, None]), 0.0)
    qdec = jnp.exp((idx + 1.0)[None, :] * lg[:, None])[:, :, None]
    kdec = jnp.exp((chunk - 1.0 - idx)[None, :] * lg[:, None])[:, :, None]
    cdec = jnp.broadcast_to(jnp.exp(chunk * lg)[:, None, None], (R_HEADS, 1, LANES))
    return dmask, qdec, kdec, cdec


def _retention(p, x, gout, wo_all, layer, batch, seq, *, chunk):
    m, n = p.shape
    d = x.shape[1]
    qk = n // (6 * R_HEADS)
    dv = 2 * qk
    tables = _retention_tables(chunk)
    full = lambda a: pl.BlockSpec(a.shape, lambda c: (0,) * a.ndim)
    rows = lambda width: pl.BlockSpec((batch, chunk, width), lambda c: (0, c, 0))
    out, state = pl.pallas_call(
        functools.partial(_retention_kernel, qk=qk, dv=dv),
        grid=(seq // chunk,),
        in_specs=[rows(n), rows(d)] + [full(t) for t in tables] + [
            full(gout), _resident((None, R_HEADS * dv, d), lambda c: (layer, 0, 0))],
        out_specs=[rows(d), pl.BlockSpec((batch, R_HEADS, qk, dv), lambda c: (0, 0, 0, 0))],
        out_shape=[jax.ShapeDtypeStruct((batch, seq, d), F32),
                   jax.ShapeDtypeStruct((batch, R_HEADS, qk, dv), F32)],
        scratch_shapes=[pltpu.VMEM((batch, chunk, R_HEADS * dv), BF16)],
        compiler_params=_cparams("arbitrary"),
        name="retention",
    )(p.reshape(batch, seq, n), x.reshape(batch, seq, d), *tables, gout, wo_all)
    return out.reshape(m, d), state


def _retention_step_kernel(p_ref, s_ref, gout_ref, y_ref, snew_ref, *, qk, dv):
    eye = (lax.broadcasted_iota(jnp.int32, (qk, qk), 0)
           == lax.broadcasted_iota(jnp.int32, (qk, qk), 1))
    for hd in range(R_HEADS):
        gamma = 1.0 - 2.0 ** (-5 - hd)
        q = p_ref[:, hd * qk:(hd + 1) * qk].astype(F32)
        k = p_ref[:, (R_HEADS + hd) * qk:(R_HEADS + hd + 1) * qk].astype(F32)
        v = p_ref[:, 2 * R_HEADS * qk + hd * dv:2 * R_HEADS * qk + (hd + 1) * dv].astype(F32)
        gate = p_ref[:, 2 * R_HEADS * qk + (R_HEADS + hd) * dv:
                     2 * R_HEADS * qk + (R_HEADS + hd + 1) * dv].astype(F32)
        qcol = jnp.sum(jnp.where(eye, q, 0.0), axis=1, keepdims=True)
        kcol = jnp.sum(jnp.where(eye, k, 0.0), axis=1, keepdims=True)
        s_prev = s_ref[hd]
        inter = jnp.sum(s_prev * qcol, axis=0, keepdims=True) * gamma
        o = inter + jnp.sum(q * k, axis=1, keepdims=True) * v
        snew_ref[hd] = s_prev * gamma + kcol * v
        y = _rms(o, gout_ref[hd]) * _silu(gate)
        y_ref[:, hd * dv:(hd + 1) * dv] = y.astype(BF16)


def _retention_step(p, state, layer, prev_out, gout):
    bd = p.shape[0]
    nl, _, nh, qk, dv = state.shape
    kernel = functools.partial(_retention_step_kernel, qk=qk, dv=dv)
    in_specs = [pl.BlockSpec((None, 1, p.shape[2]), lambda b: (b, 0, 0)),
                pl.BlockSpec((None, None, nh, qk, dv), lambda b: (layer, b, 0, 0, 0)),
                pl.BlockSpec((nh, 1, dv), lambda b: (0, 0, 0))]
    args = [p, state, gout]
    aliases = {}
    if prev_out is not None:
        in_specs.append(pl.BlockSpec(memory_space=pl.ANY))
        args.append(prev_out)
        aliases = {3: 1}
        kernel = functools.partial(_drop_arg, kernel, 3)
    return pl.pallas_call(
        kernel,
        grid=(bd,),
        in_specs=in_specs,
        out_specs=[pl.BlockSpec((None, 1, nh * dv), lambda b: (b, 0, 0)),
                   pl.BlockSpec((None, None, nh, qk, dv), lambda b: (layer, b, 0, 0, 0))],
        out_shape=[jax.ShapeDtypeStruct((bd, 1, nh * dv), BF16),
                   jax.ShapeDtypeStruct(state.shape, F32)],
        input_output_aliases=aliases,
        compiler_params=_cparams("arbitrary"),
        name="retention_step",
    )(*args)


def _proj_res_kernel(y_ref, x_ref, w_ref, out_ref):
    out_ref[...] = x_ref[...] + _dot(y_ref[...], w_ref[...])


def _proj_res(y, x, w_all, layer, *, tm):
    m, d = x.shape
    kdim = y.shape[1]
    row = lambda i: (i, 0)
    return pl.pallas_call(
        _proj_res_kernel,
        grid=(m // tm,),
        in_specs=[pl.BlockSpec((tm, kdim), row), pl.BlockSpec((tm, d), row),
                  _resident((None, kdim, d), lambda i: (layer, 0, 0))],
        out_specs=pl.BlockSpec((tm, d), row),
        out_shape=jax.ShapeDtypeStruct((m, d), F32),
        compiler_params=_cparams("arbitrary"),
        name="proj_res",
    )(y, x, w_all)


def _rope_tables(pos, half, width, sign_fold):
    lane = jnp.arange(width)
    inv = ROPE_THETA ** (-(lane % half).astype(F32) / half)
    ang = pos.astype(F32)[:, None] * inv[None, :]
    cos, sin = jnp.cos(ang), jnp.sin(ang)
    if sign_fold:
        sin = jnp.where((lane % (2 * half)) < half, -sin, sin)
    return cos, sin


def kernel(x_prompt, x_sample, cache_kv_g0, cache_kv_g1, cache_kv_g2, state_retention,
           mix_norm, ffn_norm, a_w_qkv, a_q_norm, a_k_norm, a_w_o,
           r_w_in, r_out_norm, r_w_o, f_w_gate, f_w_up, f_w_down):
    batch, seq, d = x_prompt.shape
    bd, dec_seq, _ = x_sample.shape
    assert dec_seq == 1
    depth = mix_norm.shape[0]
    caches = (cache_kv_g0, cache_kv_g1, cache_kv_g2)
    keeps = tuple(min(w, seq) for w in WINDOWS)
    tm = ROW_TILE
    qk_dim = r_w_in.shape[2] // (6 * R_HEADS)

    xp = x_prompt.reshape(batch * seq, d)
    xs = x_sample.reshape(bd, d)

    caches_t = [jnp.transpose(c, (0, 1, 3, 4, 5, 2)).reshape(c.shape[0], bd, 2 * A_WIDTH, c.shape[2])
                for c in caches]

    a_w_qkv, a_w_o, r_w_in, r_w_o, f_w_gate, f_w_up, f_w_down = (
        w.astype(BF16) for w in (a_w_qkv, a_w_o, r_w_in, r_w_o, f_w_gate, f_w_up, f_w_down))

    pos_p = jnp.arange(seq)
    pos_s = jnp.full((bd,), PAST_LEN)
    a_tabs_p = _rope_tables(pos_p, A_HEAD_DIM // 2, LANES, True)
    a_tabs_s = _rope_tables(pos_s, A_HEAD_DIM // 2, LANES, True)
    r_tabs_p = _rope_tables(pos_p, qk_dim // 2, qk_dim // 2, False)
    r_tabs_s = _rope_tables(pos_s, qk_dim // 2, qk_dim // 2, False)
    lane = jnp.arange(MXU_DIM)
    ones_bd = (lane[:, None] // A_HEAD_DIM == lane[None, :] // A_HEAD_DIM).astype(BF16)
    no_dil = (1,) * N_GROUPS

    kv_p = [[] for _ in range(N_GROUPS)]
    kv_s = [None] * N_GROUPS
    ret_p, ret_s = [], None
    for i in range(depth):
        j = i // 2
        gmix = mix_norm[i][None, :]
        if i % 2 == 0:
            gq = jnp.tile(a_q_norm[j], (1, A_HEADS))[:, None, :]
            gk = jnp.tile(a_k_norm[j], (1, A_HEADS))[:, None, :]
            qkv, kvt = _a_proj(xp, gmix, a_w_qkv, j, gq, gk, *a_tabs_p, ones_bd, tm=tm, seq=seq,
                               keeps=keeps, dils=DILATIONS, transposed_q=False)
            os_, lses = [], []
            for g in range(N_GROUPS):
                o, lse = _band_attn(*qkv[3 * g:3 * g + 3], g, batch, seq)
                os_.append(o)
                lses.append(lse)
                kv_p[g].append(kvt[g])
            xp = _a_out(os_, lses, xp, a_w_o, j, tm=tm, dils=DILATIONS)
            qts, newts = _a_proj(xs, gmix, a_w_qkv, j, gq, gk, *a_tabs_s, ones_bd, tm=bd, seq=bd,
                                 keeps=(bd,) * N_GROUPS, dils=no_dil, transposed_q=True)
            os_, lses = [], []
            for g in range(N_GROUPS):
                kv_s[g], ot, lset = _decode_attn(qts[g], newts[g][0], caches_t[g], j, kv_s[g], g)
                os_.append(ot.T)
                lses.append(jnp.repeat(lset.T, A_HEAD_DIM, axis=1))
            xs = _a_out(os_, lses, xs, a_w_o, j, tm=bd, dils=no_dil)
        else:
            gout = r_out_norm[j][:, None, :]
            p = _r_proj(xp, gmix, r_w_in, j, *r_tabs_p, tm=tm)
            xp, s_fin = _retention(p, xp, gout, r_w_o, j, batch, seq, chunk=R_CHUNK)
            ret_p.append(s_fin)
            ps = _r_proj(xs, gmix, r_w_in, j, *r_tabs_s, tm=bd)
            ys, ret_s = _retention_step(ps[:, None, :], state_retention, j, ret_s, gout)
            xs = _proj_res(ys.reshape(bd, -1), xs, r_w_o, j, tm=bd)
        gffn = ffn_norm[i][None, :]
        xp = _ffn(xp, gffn, f_w_gate, f_w_up, f_w_down, i, tm=tm)
        xs = _ffn(xs, gffn, f_w_gate, f_w_up, f_w_down, i, tm=bd)

    def rows_major(t):
        lead, slots = t.shape[:-2], t.shape[-1]
        t = t.reshape(lead + (2, A_HEADS, A_HEAD_DIM, slots))
        return jnp.moveaxis(t, -1, len(lead))

    return (xp.reshape(batch, seq, d), xs.reshape(bd, 1, d),
            rows_major(jnp.stack(kv_p[0])), rows_major(kv_s[0]),
            rows_major(jnp.stack(kv_p[1])), rows_major(kv_s[1]),
            rows_major(jnp.stack(kv_p[2])), rows_major(kv_s[2]),
            jnp.stack(ret_p), ret_s)
```

```python
import functools

import jax
import jax.numpy as jnp
from jax import lax
from jax.experimental import pallas as pl
from jax.experimental.pallas import tpu as pltpu

F32 = jnp.float32
BF16 = jnp.bfloat16

EPS = 1e-6
ROPE_THETA = 10000.0
WINDOWS = (128, 512, 2048)
DILATIONS = (1, 4, 16)
N_GROUPS = 3
A_HEADS = 8
A_HEAD_DIM = 64
A_WIDTH = A_HEADS * A_HEAD_DIM
A_BLOCK = 128
R_HEADS = 4
R_CHUNK = 256
PAST_LEN = 2048
NEG = -1e30

VMEM_LIMIT_BYTES = 56 * 1024 * 1024
LANES = 128
MXU_DIM = 256
ROW_TILE = 512
A_OUT_TILE = 256
DECODE_SEQS_PER_STEP = (4, 4, 1)
RETENTION_SEQS_PER_STEP = 2


def _cparams(*sem):
    return pltpu.CompilerParams(dimension_semantics=sem, vmem_limit_bytes=VMEM_LIMIT_BYTES)


def _resident(shape, index_map):
    return pl.BlockSpec(shape, index_map, pipeline_mode=pl.Buffered(1))


def _rms(x, g):
    ms = jnp.mean(x * x, axis=-1, keepdims=True)
    return x * lax.rsqrt(ms + EPS) * g


def _silu(x):
    return x * (1.0 / (1.0 + jnp.exp(-x)))


def _dot(a, b):
    return jnp.dot(a, b, preferred_element_type=F32)


def _dot_nt(a, b):
    return lax.dot_general(a, b, (((1,), (1,)), ((), ())), preferred_element_type=F32)


def _dot_tn(a, b):
    return lax.dot_general(a, b, (((0,), (0,)), ((), ())), preferred_element_type=F32)


def _a_proj_kernel(x_ref, gmix_ref, w_ref, gq_ref, gk_ref, cos_ref, sin_ref, ones_ref, *refs,
                   dils, tail_blocks, tiles_per_seq, transposed_q):
    n_qkv = N_GROUPS if transposed_q else 3 * N_GROUPS
    qkv_refs, kvt_refs, deint = refs[:n_qkv], refs[n_qkv:n_qkv + N_GROUPS], refs[n_qkv + N_GROUPS]
    tm = x_ref.shape[0]
    tile_in_seq = pl.program_id(0) % tiles_per_seq
    h = _rms(x_ref[...], gmix_ref[...]).astype(BF16)
    cos = jnp.concatenate([cos_ref[...]] * (A_WIDTH // LANES), axis=1)
    sin = jnp.concatenate([sin_ref[...]] * (A_WIDTH // LANES), axis=1)
    lane = lax.broadcasted_iota(jnp.int32, (1, A_WIDTH), 1)
    first_half = (lane % A_HEAD_DIM) < (A_HEAD_DIM // 2)
    ones = ones_ref[...]

    def normed_rope(p, gain):
        y = (p * p).astype(BF16)
        ss = jnp.concatenate([_dot(y[:, :MXU_DIM], ones), _dot(y[:, MXU_DIM:], ones)], axis=1)
        pn = p * lax.rsqrt(ss * (1.0 / A_HEAD_DIM) + EPS) * gain
        rot = jnp.where(first_half,
                        pltpu.roll(pn, A_WIDTH - A_HEAD_DIM // 2, 1),
                        pltpu.roll(pn, A_HEAD_DIM // 2, 1))
        return pn * cos + rot * sin

    def store_dilated(out_ref, slot, val, dil):
        if dil == 1:
            out_ref[...] = val.astype(BF16)
            return
        tiles = A_WIDTH // LANES
        for c in range(tiles):
            deint[slot * tiles + c] = val[:, c * LANES:(c + 1) * LANES]
        for r in range(dil):
            for c in range(tiles):
                out_ref[:, r * A_WIDTH + c * LANES:r * A_WIDTH + (c + 1) * LANES] = (
                    deint[slot * tiles + c, pl.ds(r, tm // dil, stride=dil), :].astype(BF16))

    for g in range(N_GROUPS):
        c0 = 3 * g * A_WIDTH
        q = normed_rope(_dot(h, w_ref[:, c0:c0 + A_WIDTH]), gq_ref[g]) * (A_HEAD_DIM ** -0.5)
        k = normed_rope(_dot(h, w_ref[:, c0 + A_WIDTH:c0 + 2 * A_WIDTH]), gk_ref[g])
        v = _dot(h, w_ref[:, c0 + 2 * A_WIDTH:c0 + 3 * A_WIDTH])
        if transposed_q:
            qkv_refs[g][...] = q.T.astype(BF16)
        else:
            store_dilated(qkv_refs[3 * g], 0, q, dils[g])
            store_dilated(qkv_refs[3 * g + 1], 1, k, dils[g])
            store_dilated(qkv_refs[3 * g + 2], 2, v, dils[g])

        kvt_ref = kvt_refs[g]
        tw = kvt_ref.shape[1]

        @pl.when(tile_in_seq >= tiles_per_seq - tail_blocks[g])
        def _(k=k, v=v, kvt_ref=kvt_ref, tw=tw):
            kvt_ref[0:A_WIDTH, :] = k[tm - tw:, :].T
            kvt_ref[A_WIDTH:, :] = v[tm - tw:, :].T


def _a_proj(x, gmix, w_all, layer, gq, gk, cos, sin, ones, *, tm, seq, keeps, dils, transposed_q):
    m, d = x.shape
    n_seq = m // seq
    tps = seq // tm
    ntab = cos.shape[0] // tm
    tws = [min(tm, kp) for kp in keeps]
    tail_blocks = tuple(kp // tw for kp, tw in zip(keeps, tws))

    def kvt_map(nb):
        return lambda i: (i // tps, 0, jnp.maximum(i % tps - (tps - nb), 0))

    if transposed_q:
        qkv_specs = [pl.BlockSpec((A_WIDTH, tm), lambda i: (0, i))] * N_GROUPS
        qkv_shapes = [jax.ShapeDtypeStruct((A_WIDTH, m), BF16)] * N_GROUPS
    else:
        qkv_specs, qkv_shapes = [], []
        for dil in dils:
            qkv_specs += [pl.BlockSpec((tm // dil, dil * A_WIDTH), lambda i: (i, 0))] * 3
            qkv_shapes += [jax.ShapeDtypeStruct((m // dil, dil * A_WIDTH), BF16)] * 3
    kvt_specs = [pl.BlockSpec((None, 2 * A_WIDTH, tw), kvt_map(nb)) for tw, nb in zip(tws, tail_blocks)]
    kvt_shapes = [jax.ShapeDtypeStruct((n_seq, 2 * A_WIDTH, kp), F32) for kp in keeps]
    gain_spec = pl.BlockSpec((N_GROUPS, 1, A_WIDTH), lambda i: (0, 0, 0))
    tab_spec = pl.BlockSpec((tm, LANES), lambda i: (i % ntab, 0))
    outs = pl.pallas_call(
        functools.partial(_a_proj_kernel, dils=dils, tail_blocks=tail_blocks, tiles_per_seq=tps,
                          transposed_q=transposed_q),
        grid=(m // tm,),
        in_specs=[
            pl.BlockSpec((tm, d), lambda i: (i, 0)),
            pl.BlockSpec((1, d), lambda i: (0, 0)),
            _resident((None, d, 3 * N_GROUPS * A_WIDTH), lambda i: (layer, 0, 0)),
            gain_spec, gain_spec, tab_spec, tab_spec,
            pl.BlockSpec((MXU_DIM, MXU_DIM), lambda i: (0, 0)),
        ],
        out_specs=qkv_specs + kvt_specs,
        out_shape=qkv_shapes + kvt_shapes,
        scratch_shapes=[pltpu.VMEM((3 * A_WIDTH // LANES, tm, LANES), F32)],
        compiler_params=_cparams("arbitrary"),
        name="a_proj",
    )(x, gmix, w_all, gq, gk, cos, sin, ones)
    return outs[:-N_GROUPS], outs[-N_GROUPS:]


def _band_attn_kernel(q_ref, kp_ref, kc_ref, vp_ref, vc_ref, o_ref, lse_ref, kb, vb, *, nsub):
    i = pl.program_id(2)
    kb[0:A_BLOCK, :] = kp_ref[...]
    kb[A_BLOCK:, :] = kc_ref[...]
    vb[0:A_BLOCK, :] = vp_ref[...]
    vb[A_BLOCK:, :] = vc_ref[...]
    heads_per_half = MXU_DIM // A_HEAD_DIM
    stacked = (heads_per_half * A_BLOCK, 2 * A_BLOCK)
    qi = lax.broadcasted_iota(jnp.int32, stacked, 0) & (A_BLOCK - 1)
    kj = lax.broadcasted_iota(jnp.int32, stacked, 1)
    band = (kj >= qi) & (kj <= qi + A_BLOCK)
    band_first = band & ((kj >= A_BLOCK) | (i > 0))
    head_of_lane = lax.broadcasted_iota(jnp.int32, (1, MXU_DIM), 1) // A_HEAD_DIM

    for s in range(nsub):
        valid = band_first if s == 0 else band
        rows = slice(s * A_BLOCK, (s + 1) * A_BLOCK)
        keys = slice(s * A_BLOCK, (s + 2) * A_BLOCK)
        for half in range(A_WIDTH // MXU_DIM):
            lanes = slice(half * MXU_DIM, (half + 1) * MXU_DIM)
            qh = q_ref[rows, lanes]
            qs = jnp.concatenate(
                [jnp.where(head_of_lane == hd, qh, jnp.zeros_like(qh))
                 for hd in range(heads_per_half)], axis=0)
            sc = _dot_nt(qs, kb[keys, lanes])
            sc = jnp.where(valid, sc, NEG)
            m = jnp.max(sc, axis=1, keepdims=True)
            p = jnp.exp(sc - m)
            l = jnp.sum(p, axis=1, keepdims=True)
            r = _dot(p.astype(BF16), vb[keys, lanes]) * (1.0 / l)
            lse = m + jnp.log(l)
            o_half = jnp.zeros((A_BLOCK, MXU_DIM), F32)
            lse_half = jnp.zeros((A_BLOCK, MXU_DIM), F32)
            for hd in range(heads_per_half):
                hrows = slice(hd * A_BLOCK, (hd + 1) * A_BLOCK)
                sel = head_of_lane == hd
                o_half = jnp.where(sel, r[hrows, :], o_half)
                lse_half = jnp.where(sel, lse[hrows, :], lse_half)
            o_ref[rows, lanes] = o_half.astype(BF16)
            lse_ref[rows, lanes] = lse_half


def _band_attn_grid(g, batch, seq):
    dil = DILATIONS[g]
    sub = seq // dil
    tq = min(4 * A_BLOCK, sub)
    nb = sub // tq
    return (batch, dil, nb), (lambda b, r, i: (b * dil + r) * nb + i)


def _band_attn(q, k, v, g, batch, seq, side=None):
    grid, _ = _band_attn_grid(g, batch, seq)
    nb = grid[2]
    tq = seq // DILATIONS[g] // nb
    nsub = tq // A_BLOCK
    cur = pl.BlockSpec((tq, A_WIDTH), lambda b, r, i: (b * nb + i, r))
    prev = pl.BlockSpec((A_BLOCK, A_WIDTH),
                        lambda b, r, i: (b * nb * nsub + jnp.maximum(i * nsub - 1, 0), r))
    return _call(
        functools.partial(_band_attn_kernel, nsub=nsub), grid,
        [cur, prev, cur, prev, cur], [cur, cur],
        [jax.ShapeDtypeStruct(q.shape, BF16), jax.ShapeDtypeStruct(q.shape, F32)],
        [pltpu.VMEM((tq + A_BLOCK, A_WIDTH), BF16), pltpu.VMEM((tq + A_BLOCK, A_WIDTH), BF16)],
        [q, k, k, v, v], f"band_attn_g{g}", side)


def _a_out_kernel(o0_ref, o1_ref, o2_ref, l0_ref, l1_ref, l2_ref, x_ref, w_ref, out_ref, scr, *, dils):
    tm = x_ref.shape[0]

    def natural(ref, slot, dil):
        if dil == 1:
            return ref[...].astype(F32)
        tiles = A_WIDTH // LANES
        for r in range(dil):
            for c in range(tiles):
                scr[slot * tiles + c, pl.ds(r, tm // dil, stride=dil), :] = (
                    ref[:, r * A_WIDTH + c * LANES:r * A_WIDTH + (c + 1) * LANES].astype(F32))
        return jnp.concatenate([scr[slot * tiles + c] for c in range(tiles)], axis=1)

    o0, o1, o2 = (natural(r, s, d) for s, (r, d) in enumerate(zip((o0_ref, o1_ref, o2_ref), dils)))
    l0, l1, l2 = (natural(r, 3 + s, d) for s, (r, d) in enumerate(zip((l0_ref, l1_ref, l2_ref), dils)))
    mx = jnp.maximum(jnp.maximum(l0, l1), l2)
    w0, w1, w2 = jnp.exp(l0 - mx), jnp.exp(l1 - mx), jnp.exp(l2 - mx)
    o = (w0 * o0 + w1 * o1 + w2 * o2) / (w0 + w1 + w2)
    out_ref[...] = x_ref[...] + _dot(o.astype(BF16), w_ref[...])


def _a_out(os_, lses, x, w_all, layer, *, tm, dils, side=None):
    m, d = x.shape
    row = lambda i: (i, 0)
    grp = [pl.BlockSpec((tm // dil, dil * A_WIDTH), row) for dil in dils]
    return _call(
        functools.partial(_a_out_kernel, dils=dils), (m // tm,),
        grp + grp + [pl.BlockSpec((tm, d), row),
                     _resident((None, A_WIDTH, d), lambda i: (layer, 0, 0))],
        [pl.BlockSpec((tm, d), row)], [jax.ShapeDtypeStruct((m, d), F32)],
        [pltpu.VMEM((2 * N_GROUPS * A_WIDTH // LANES, tm, LANES), F32)],
        [*os_, *lses, x, w_all], "a_out", side)


def _split3(x):
    hi = x.astype(BF16)
    r1 = x - hi.astype(F32)
    mid = r1.astype(BF16)
    lo = (r1 - mid.astype(F32)).astype(BF16)
    return hi, mid, lo


def _decode_attn_body(qt_ref, newt_ref, c_ref, out_ref, ot_ref, lset_ref, first_seq, *, dil, n_rel):
    per_step, _, lg = c_ref.shape
    bd = qt_ref.shape[1]
    reps = lg // LANES
    lane_seq = lax.broadcasted_iota(jnp.int32, (1, bd), 1)
    row_seq = lax.broadcasted_iota(jnp.int32, (bd, LANES), 0)
    head_row = lax.broadcasted_iota(jnp.int32, (A_HEADS, 1), 0)
    pos = lax.broadcasted_iota(jnp.int32, (1, lg), 1)
    back = lg - pos
    valid = ((back & (dil - 1)) == 0) & (back <= n_rel * dil)
    last = pos == lg - 1

    def lanes(x):
        return x if reps == 1 else jnp.concatenate([x] * reps, axis=1)

    def head_rows(col):
        return jnp.concatenate(
            [jnp.broadcast_to(col[hd:hd + 1, :], (A_HEAD_DIM, 1)) for hd in range(A_HEADS)], axis=0)

    krows = [slice(hd * A_HEAD_DIM, (hd + 1) * A_HEAD_DIM) for hd in range(A_HEADS)]
    vrows = [slice(A_WIDTH + hd * A_HEAD_DIM, A_WIDTH + (hd + 1) * A_HEAD_DIM) for hd in range(A_HEADS)]
    seqs = [first_seq + e for e in range(per_step)]
    qt = qt_ref[...]
    new_hi, new_mid, new_lo = _split3(newt_ref[...])

    qbs, newbs = [], []
    for b in seqs:
        pick = (row_seq == b).astype(BF16)
        qbs.append(_dot(qt, pick))
        newbs.append(_dot(new_hi, pick) + _dot(new_mid, pick) + _dot(new_lo, pick))

    scs, sc_news = [], []
    for e in range(per_step):
        sc = jnp.zeros((A_HEADS, lg), F32)
        sc_new = jnp.zeros((A_HEADS, LANES), F32)
        for hd in range(A_HEADS):
            qh = qbs[e][krows[hd], :]
            sc = jnp.where(head_row == hd,
                           jnp.sum(c_ref[e, krows[hd], :] * lanes(qh), axis=0, keepdims=True), sc)
            sc_new = jnp.where(head_row == hd,
                               jnp.sum(newbs[e][krows[hd], :] * qh, axis=0, keepdims=True), sc_new)
        scs.append(jnp.where(valid, sc, NEG))
        sc_news.append(sc_new[:, 0:1])

    ps, p_news, ls, ms = [], [], [], []
    for e in range(per_step):
        m = jnp.maximum(jnp.max(scs[e], axis=1, keepdims=True), sc_news[e])
        p = jnp.where(valid, jnp.exp(scs[e] - m), 0.0)
        p_new = jnp.exp(sc_news[e] - m)
        ps.append(p)
        p_news.append(p_new)
        ms.append(m)
        ls.append(jnp.sum(p, axis=1, keepdims=True) + p_new)

    for e in range(per_step):
        parts = []
        for hd in range(A_HEADS):
            kt = c_ref[e, krows[hd], :]
            vt = c_ref[e, vrows[hd], :]
            w = vt * ps[e][hd:hd + 1, :]
            part = w[:, 0:LANES]
            for t in range(1, reps):
                part = part + w[:, t * LANES:(t + 1) * LANES]
            parts.append(part)
            out_ref[e, krows[hd], :] = jnp.where(
                last, lanes(newbs[e][krows[hd], :]), pltpu.roll(kt, lg - 1, 1))
            out_ref[e, vrows[hd], :] = jnp.where(
                last, lanes(newbs[e][vrows[hd], :]), pltpu.roll(vt, lg - 1, 1))
        acc = jnp.sum(jnp.concatenate(parts, axis=0), axis=1, keepdims=True)
        o = (acc + head_rows(p_news[e]) * newbs[e][A_WIDTH:, 0:1]) / head_rows(ls[e])
        onehot = lane_seq == seqs[e]
        ot_ref[...] = jnp.where(onehot, o, ot_ref[...])
        lset_ref[...] = jnp.where(onehot, ms[e] + jnp.log(ls[e]), lset_ref[...])


class _DecodeSide:
    def __init__(self, qt, newt, cache, layer, carry, g, first_seq, count, grid, lin):
        nl, bd, rows, lg = cache.shape
        per_step = DECODE_SEQS_PER_STEP[g]
        assert count % per_step == 0 and first_seq % per_step == 0
        n = count // per_step
        host_steps = 1
        for s in grid:
            host_steps *= s
        assert n <= host_steps
        kv_prev, ot_prev, lse_prev = carry
        const = lambda *idx: (0, 0)
        buf_spec = pl.BlockSpec(
            (None, per_step, rows, lg),
            lambda *idx: (layer, first_seq // per_step + jnp.minimum(lin(*idx), n - 1), 0, 0))
        ot_spec = pl.BlockSpec((A_WIDTH, bd), const)
        lse_spec = pl.BlockSpec((A_HEADS, bd), const)
        self.in_specs = [pl.BlockSpec((A_WIDTH, bd), const), pl.BlockSpec((rows, bd), const),
                         buf_spec, ot_spec, lse_spec]
        self.args = [qt, newt, cache, ot_prev, lse_prev]
        self.aliases = {}
        if kv_prev is not None:
            self.in_specs.append(pl.BlockSpec(memory_space=pl.ANY))
            self.args.append(kv_prev)
            self.aliases = {5: 0}
        self.out_specs = [buf_spec, ot_spec, lse_spec]
        self.out_shape = [jax.ShapeDtypeStruct(cache.shape, F32),
                          jax.ShapeDtypeStruct((A_WIDTH, bd), F32),
                          jax.ShapeDtypeStruct((A_HEADS, bd), F32)]

        def body(ins, outs):
            qt_ref, newt_ref, c_ref, ot_in, lse_in = ins[:5]
            out_ref, ot_ref, lset_ref = outs
            step = lin(*[pl.program_id(a) for a in range(len(grid))])

            @pl.when(step == 0)
            def _():
                ot_ref[...] = ot_in[...]
                lset_ref[...] = lse_in[...]

            def work():
                _decode_attn_body(qt_ref, newt_ref, c_ref, out_ref, ot_ref, lset_ref,
                                  first_seq + step * per_step,
                                  dil=DILATIONS[g], n_rel=WINDOWS[g] // DILATIONS[g])

            if n < host_steps:
                pl.when(step < n)(work)
            else:
                work()

        self.body = body


def _call(host_kernel, grid, in_specs, out_specs, out_shape, scratch_shapes, args, name, side=None):
    n_in, n_out = len(in_specs), len(out_specs)
    if side is None:
        kernel, aliases, side_args = host_kernel, {}, []
    else:
        n_sin, n_sout = len(side.in_specs), len(side.out_specs)

        def kernel(*refs):
            o0 = n_in + n_sin
            if host_kernel is not None:
                host_kernel(*refs[:n_in], *refs[o0:o0 + n_out], *refs[o0 + n_out + n_sout:])
            side.body(refs[n_in:o0], refs[o0 + n_out:o0 + n_out + n_sout])

        aliases = {n_in + a: n_out + b for a, b in side.aliases.items()}
        in_specs, side_args = in_specs + side.in_specs, side.args
        out_specs, out_shape = out_specs + side.out_specs, out_shape + side.out_shape
    outs = pl.pallas_call(
        kernel, grid=grid, in_specs=in_specs, out_specs=out_specs, out_shape=out_shape,
        scratch_shapes=scratch_shapes, input_output_aliases=aliases,
        compiler_params=_cparams(*(("arbitrary",) * len(grid))), name=name,
    )(*args, *side_args)
    return outs[:n_out], outs[n_out:]


def _drop_arg(kernel, idx, *refs):
    return kernel(*refs[:idx], *refs[idx + 1:])


def _ffn_kernel(x_ref, g_ref, wg_ref, wu_ref, wd_ref, out_ref, *, chunk):
    x = x_ref[...]
    h = _rms(x, g_ref[...]).astype(BF16)
    out_ref[...] = x
    for c in range(wg_ref.shape[1] // chunk):
        cols = slice(c * chunk, (c + 1) * chunk)
        gate = _dot(h, wg_ref[:, cols])
        up = _dot(h, wu_ref[:, cols])
        out_ref[...] += _dot((_silu(gate) * up).astype(BF16), wd_ref[cols, :])


def _ffn(x, g, wg_all, wu_all, wd_all, layer, *, tm):
    m, d = x.shape
    f = wg_all.shape[2]
    row = lambda i: (i, 0)
    sel = lambda i: (layer, 0, 0)
    return pl.pallas_call(
        functools.partial(_ffn_kernel, chunk=MXU_DIM),
        grid=(m // tm,),
        in_specs=[pl.BlockSpec((tm, d), row), pl.BlockSpec((1, d), lambda i: (0, 0)),
                  _resident((None, d, f), sel), _resident((None, d, f), sel),
                  _resident((None, f, d), sel)],
        out_specs=pl.BlockSpec((tm, d), row),
        out_shape=jax.ShapeDtypeStruct((m, d), F32),
        compiler_params=_cparams("arbitrary"),
        name="ffn",
    )(x, g, wg_all, wu_all, wd_all)


def _r_proj_kernel(x_ref, gmix_ref, w_ref, cos_ref, sin_ref, out_ref, *qkt_ref, qk_dim):
    h = _rms(x_ref[...], gmix_ref[...]).astype(BF16)
    half = qk_dim // 2
    nqk = 2 * R_HEADS * qk_dim
    cos, sin = cos_ref[...], sin_ref[...]
    p = _dot(h, w_ref[:, 0:nqk])
    for hd in range(2 * R_HEADS):
        scale = 1.0 if hd < R_HEADS else qk_dim ** -0.5
        c0 = hd * qk_dim
        x1, x2 = p[:, c0:c0 + half], p[:, c0 + half:c0 + qk_dim]
        o1 = ((x1 * cos - x2 * sin) * scale).astype(BF16)
        o2 = ((x2 * cos + x1 * sin) * scale).astype(BF16)
        out_ref[:, c0:c0 + half] = o1
        out_ref[:, c0 + half:c0 + qk_dim] = o2
        if qkt_ref:
            qkt_ref[0][c0:c0 + half, :] = o1.astype(F32).T.astype(BF16)
            qkt_ref[0][c0 + half:c0 + qk_dim, :] = o2.astype(F32).T.astype(BF16)
    for c0 in range(nqk, w_ref.shape[1], nqk):
        out_ref[:, c0:c0 + nqk] = _dot(h, w_ref[:, c0:c0 + nqk]).astype(BF16)


def _r_proj(x, gmix, w_all, layer, cos, sin, *, tm, transposed_qk=False):
    m, d = x.shape
    n = w_all.shape[2]
    qk_dim = n // (6 * R_HEADS)
    nqk = 2 * R_HEADS * qk_dim
    ntab = cos.shape[0] // tm
    tab = pl.BlockSpec((tm, qk_dim // 2), lambda i: (i % ntab, 0))
    out_specs = [pl.BlockSpec((tm, n), lambda i: (i, 0))]
    out_shape = [jax.ShapeDtypeStruct((m, n), BF16)]
    if transposed_qk:
        out_specs.append(pl.BlockSpec((nqk, tm), lambda i: (0, i)))
        out_shape.append(jax.ShapeDtypeStruct((nqk, m), BF16))
    return pl.pallas_call(
        functools.partial(_r_proj_kernel, qk_dim=qk_dim),
        grid=(m // tm,),
        in_specs=[pl.BlockSpec((tm, d), lambda i: (i, 0)),
                  pl.BlockSpec((1, d), lambda i: (0, 0)),
                  _resident((None, d, n), lambda i: (layer, 0, 0)), tab, tab],
        out_specs=out_specs,
        out_shape=out_shape,
        compiler_params=_cparams("arbitrary"),
        name="r_proj",
    )(x, gmix, w_all, cos, sin)


def _retention_kernel(p_ref, x_ref, dmask_ref, qdec_ref, kdec_ref, cdec_ref, gout_ref, wo_ref,
                      out_ref, state_ref, y_scr, *, qk, dv):
    c = pl.program_id(0)

    @pl.when(c == 0)
    def _():
        state_ref[...] = jnp.zeros_like(state_ref)

    v0 = 2 * R_HEADS * qk
    g0 = v0 + R_HEADS * dv
    for b in range(p_ref.shape[0]):
        for hd in range(R_HEADS):
            q = p_ref[b, :, hd * qk:(hd + 1) * qk]
            k = p_ref[b, :, (R_HEADS + hd) * qk:(R_HEADS + hd + 1) * qk]
            v = p_ref[b, :, v0 + hd * dv:v0 + (hd + 1) * dv]
            gate = p_ref[b, :, g0 + hd * dv:g0 + (hd + 1) * dv].astype(F32)
            s_prev = state_ref[b, hd]
            inter = _dot(q, s_prev.astype(BF16)) * qdec_ref[hd]
            scores = _dot_nt(q, k) * dmask_ref[hd]
            o = inter + _dot(scores.astype(BF16), v)
            kd = (k.astype(F32) * kdec_ref[hd]).astype(BF16)
            state_ref[b, hd] = s_prev * cdec_ref[hd][:, 0:1] + _dot_tn(kd, v)
            y = _rms(o, gout_ref[hd]) * _silu(gate)
            y_scr[b, :, hd * dv:(hd + 1) * dv] = y.astype(BF16)
        out_ref[b] = x_ref[b] + _dot(y_scr[b], wo_ref[...])


def _retention_tables(chunk):
    lg = jnp.log1p(-jnp.exp2(-5.0 - jnp.arange(R_HEADS, dtype=F32)))
    idx = jnp.arange(chunk, dtype=F32)
    rel = idx[:, None] - idx[None, :]
    dmask = jnp.where(rel >= 0, jnp.exp(jnp.maximum(rel, 0.0) * lg[:, None, None]), 0.0)
    qdec = jnp.exp((idx + 1.0)[None, :] * lg[:, None])[:, :, None]
    kdec = jnp.exp((chunk - 1.0 - idx)[None, :] * lg[:, None])[:, :, None]
    cdec = jnp.broadcast_to(jnp.exp(chunk * lg)[:, None, None], (R_HEADS, 1, LANES))
    return dmask, qdec, kdec, cdec


def _retention(p, x, gout, wo_all, layer, batch, seq, *, chunk):
    m, n = p.shape
    d = x.shape[1]
    qk = n // (6 * R_HEADS)
    dv = 2 * qk
    tables = _retention_tables(chunk)
    full = lambda a: pl.BlockSpec(a.shape, lambda c: (0,) * a.ndim)
    rows = lambda width: pl.BlockSpec((batch, chunk, width), lambda c: (0, c, 0))
    out, state = pl.pallas_call(
        functools.partial(_retention_kernel, qk=qk, dv=dv),
        grid=(seq // chunk,),
        in_specs=[rows(n), rows(d)] + [full(t) for t in tables] + [
            full(gout), _resident((None, R_HEADS * dv, d), lambda c: (layer, 0, 0))],
        out_specs=[rows(d), pl.BlockSpec((batch, R_HEADS, qk, dv), lambda c: (0, 0, 0, 0))],
        out_shape=[jax.ShapeDtypeStruct((batch, seq, d), F32),
                   jax.ShapeDtypeStruct((batch, R_HEADS, qk, dv), F32)],
        scratch_shapes=[pltpu.VMEM((batch, chunk, R_HEADS * dv), BF16)],
        compiler_params=_cparams("arbitrary"),
        name="retention",
    )(p.reshape(batch, seq, n), x.reshape(batch, seq, d), *tables, gout, wo_all)
    return out.reshape(m, d), state


def _retention_step_kernel(p_ref, qkt_ref, s_ref, gout_ref, y_ref, snew_ref, *, qk, dv):
    step = pl.program_id(0)
    per_step = s_ref.shape[0]
    bd = qkt_ref.shape[1]
    reps = dv // LANES
    row_seq = lax.broadcasted_iota(jnp.int32, (bd, LANES), 0)
    qkt = qkt_ref[...]
    v0 = 2 * R_HEADS * qk
    g0 = v0 + R_HEADS * dv

    def lanes(x):
        return jnp.concatenate([x] * reps, axis=1)

    for e in range(per_step):
        pick = (row_seq == step * per_step + e).astype(BF16)
        qkb = _dot(qkt, pick)
        for hd in range(R_HEADS):
            gamma = 1.0 - 2.0 ** (-5 - hd)
            qb = qkb[hd * qk:(hd + 1) * qk, :]
            kb = qkb[(R_HEADS + hd) * qk:(R_HEADS + hd + 1) * qk, :]
            v = p_ref[e, :, v0 + hd * dv:v0 + (hd + 1) * dv].astype(F32)
            gate = p_ref[e, :, g0 + hd * dv:g0 + (hd + 1) * dv].astype(F32)
            s_prev = s_ref[e, hd]
            inter = jnp.sum(s_prev * lanes(qb), axis=0, keepdims=True) * gamma
            qk_dot = jnp.sum(qb * kb, axis=0, keepdims=True)
            o = inter + lanes(qk_dot) * v
            snew_ref[e, hd] = s_prev * gamma + lanes(kb) * v
            y = _rms(o, gout_ref[hd]) * _silu(gate)
            y_ref[e, :, hd * dv:(hd + 1) * dv] = y.astype(BF16)


def _retention_step(p, qkt, state, layer, prev_out, gout, *, per_step):
    bd = p.shape[0]
    nl, _, nh, qk, dv = state.shape
    kernel = functools.partial(_retention_step_kernel, qk=qk, dv=dv)
    state_spec = pl.BlockSpec((None, per_step, nh, qk, dv), lambda b: (layer, b, 0, 0, 0))
    in_specs = [pl.BlockSpec((per_step, 1, p.shape[2]), lambda b: (b, 0, 0)),
                pl.BlockSpec(qkt.shape, lambda b: (0, 0)),
                state_spec,
                pl.BlockSpec((nh, 1, dv), lambda b: (0, 0, 0))]
    args = [p, qkt, state, gout]
    aliases = {}
    if prev_out is not None:
        in_specs.append(pl.BlockSpec(memory_space=pl.ANY))
        args.append(prev_out)
        aliases = {4: 1}
        kernel = functools.partial(_drop_arg, kernel, 4)
    return pl.pallas_call(
        kernel,
        grid=(bd // per_step,),
        in_specs=in_specs,
        out_specs=[pl.BlockSpec((per_step, 1, nh * dv), lambda b: (b, 0, 0)), state_spec],
        out_shape=[jax.ShapeDtypeStruct((bd, 1, nh * dv), BF16),
                   jax.ShapeDtypeStruct(state.shape, F32)],
        input_output_aliases=aliases,
        compiler_params=_cparams("arbitrary"),
        name="retention_step",
    )(*args)


def _proj_res_kernel(y_ref, x_ref, w_ref, out_ref):
    out_ref[...] = x_ref[...] + _dot(y_ref[...], w_ref[...])


def _proj_res(y, x, w_all, layer, *, tm):
    m, d = x.shape
    kdim = y.shape[1]
    row = lambda i: (i, 0)
    return pl.pallas_call(
        _proj_res_kernel,
        grid=(m // tm,),
        in_specs=[pl.BlockSpec((tm, kdim), row), pl.BlockSpec((tm, d), row),
                  _resident((None, kdim, d), lambda i: (layer, 0, 0))],
        out_specs=pl.BlockSpec((tm, d), row),
        out_shape=jax.ShapeDtypeStruct((m, d), F32),
        compiler_params=_cparams("arbitrary"),
        name="proj_res",
    )(y, x, w_all)


def _rope_tables(pos, half, width, sign_fold):
    lane = jnp.arange(width)
    inv = ROPE_THETA ** (-(lane % half).astype(F32) / half)
    ang = pos.astype(F32)[:, None] * inv[None, :]
    cos, sin = jnp.cos(ang), jnp.sin(ang)
    if sign_fold:
        sin = jnp.where((lane % (2 * half)) < half, -sin, sin)
    return cos, sin


def kernel(x_prompt, x_sample, cache_kv_g0, cache_kv_g1, cache_kv_g2, state_retention,
           mix_norm, ffn_norm, a_w_qkv, a_q_norm, a_k_norm, a_w_o,
           r_w_in, r_out_norm, r_w_o, f_w_gate, f_w_up, f_w_down):
    batch, seq, d = x_prompt.shape
    bd, dec_seq, _ = x_sample.shape
    assert dec_seq == 1
    depth = mix_norm.shape[0]
    caches = (cache_kv_g0, cache_kv_g1, cache_kv_g2)
    keeps = tuple(min(w, seq) for w in WINDOWS)
    tm = ROW_TILE
    qk_dim = r_w_in.shape[2] // (6 * R_HEADS)

    xp = x_prompt.reshape(batch * seq, d)
    xs = x_sample.reshape(bd, d)

    caches_t = [jnp.transpose(c, (0, 1, 3, 4, 5, 2)).reshape(c.shape[0], bd, 2 * A_WIDTH, c.shape[2])
                for c in caches]

    a_w_qkv, a_w_o, r_w_in, r_w_o, f_w_gate, f_w_up, f_w_down = (
        w.astype(BF16) for w in (a_w_qkv, a_w_o, r_w_in, r_w_o, f_w_gate, f_w_up, f_w_down))

    pos_p = jnp.arange(seq)
    pos_s = jnp.full((bd,), PAST_LEN)
    a_tabs_p = _rope_tables(pos_p, A_HEAD_DIM // 2, LANES, True)
    a_tabs_s = _rope_tables(pos_s, A_HEAD_DIM // 2, LANES, True)
    r_tabs_p = _rope_tables(pos_p, qk_dim // 2, qk_dim // 2, False)
    r_tabs_s = _rope_tables(pos_s, qk_dim // 2, qk_dim // 2, False)
    lane = jnp.arange(MXU_DIM)
    ones_bd = (lane[:, None] // A_HEAD_DIM == lane[None, :] // A_HEAD_DIM).astype(BF16)
    no_dil = (1,) * N_GROUPS

    kv_p = [[] for _ in range(N_GROUPS)]
    kv_s = [None] * N_GROUPS
    ret_p, ret_s = [], None
    for i in range(depth):
        j = i // 2
        gmix = mix_norm[i][None, :]
        if i % 2 == 0:
            gq = jnp.tile(a_q_norm[j], (1, A_HEADS))[:, None, :]
            gk = jnp.tile(a_k_norm[j], (1, A_HEADS))[:, None, :]
            qkv, kvt = _a_proj(xp, gmix, a_w_qkv, j, gq, gk, *a_tabs_p, ones_bd, tm=tm, seq=seq,
                               keeps=keeps, dils=DILATIONS, transposed_q=False)
            qts, newts = _a_proj(xs, gmix, a_w_qkv, j, gq, gk, *a_tabs_s, ones_bd, tm=bd, seq=bd,
                                 keeps=(bd,) * N_GROUPS, dils=no_dil, transposed_q=True)
            g_big = N_GROUPS - 1
            carries = [(kv_s[g], jnp.zeros((A_WIDTH, bd), F32), jnp.zeros((A_HEADS, bd), F32))
                       for g in range(N_GROUPS)]
            done = 0

            def ride(grid, lin):
                nonlocal done
                steps = 1
                for s in grid:
                    steps *= s
                count = min(steps * DECODE_SEQS_PER_STEP[g_big], bd - done)
                if count == 0:
                    return None
                side = _DecodeSide(qts[g_big], newts[g_big][0], caches_t[g_big], j, carries[g_big],
                                   g_big, done, count, grid, lin)
                done += count
                return side

            os_, lses = [], []
            for g in range(N_GROUPS):
                side = ride(*_band_attn_grid(g, batch, seq))
                (o, lse), rode = _band_attn(*qkv[3 * g:3 * g + 3], g, batch, seq, side)
                if side is not None:
                    carries[g_big] = tuple(rode)
                os_.append(o)
                lses.append(lse)
                kv_p[g].append(kvt[g])
            tm_out = A_OUT_TILE
            side = ride((batch * seq // tm_out,), lambda i: i)
            (xp,), rode = _a_out(os_, lses, xp, a_w_o, j, tm=tm_out, dils=DILATIONS, side=side)
            if side is not None:
                carries[g_big] = tuple(rode)
            os_, lses = [], []
            for g in range(N_GROUPS):
                first = done if g == g_big else 0
                if first < bd:
                    grid = ((bd - first) // DECODE_SEQS_PER_STEP[g],)
                    side = _DecodeSide(qts[g], newts[g][0], caches_t[g], j, carries[g], g, first,
                                       bd - first, grid, lambda s: s)
                    _, rode = _call(None, grid, [], [], [], [], [], f"decode_attn_g{g}", side)
                    carries[g] = tuple(rode)
                kv_s[g], ot, lset = carries[g]
                os_.append(ot.T)
                lses.append(jnp.repeat(lset.T, A_HEAD_DIM, axis=1))
            (xs,), _ = _a_out(os_, lses, xs, a_w_o, j, tm=bd, dils=no_dil)
        else:
            gout = r_out_norm[j][:, None, :]
            p, = _r_proj(xp, gmix, r_w_in, j, *r_tabs_p, tm=tm)
            xp, s_fin = _retention(p, xp, gout, r_w_o, j, batch, seq, chunk=R_CHUNK)
            ret_p.append(s_fin)
            ps, qkt = _r_proj(xs, gmix, r_w_in, j, *r_tabs_s, tm=bd, transposed_qk=True)
            ys, ret_s = _retention_step(ps[:, None, :], qkt, state_retention, j, ret_s, gout,
                                        per_step=RETENTION_SEQS_PER_STEP)
            xs = _proj_res(ys.reshape(bd, -1), xs, r_w_o, j, tm=bd)
        gffn = ffn_norm[i][None, :]
        xp = _ffn(xp, gffn, f_w_gate, f_w_up, f_w_down, i, tm=tm)
        xs = _ffn(xs, gffn, f_w_gate, f_w_up, f_w_down, i, tm=bd)

    def rows_major(t):
        lead, slots = t.shape[:-2], t.shape[-1]
        t = t.reshape(lead + (2, A_HEADS, A_HEAD_DIM, slots))
        return jnp.moveaxis(t, -1, len(lead))

    return (xp.reshape(batch, seq, d), xs.reshape(bd, 1, d),
            rows_major(jnp.stack(kv_p[0])), rows_major(kv_s[0]),
            rows_major(jnp.stack(kv_p[1])), rows_major(kv_s[1]),
            rows_major(jnp.stack(kv_p[2])), rows_major(kv_s[2]),
            jnp.stack(ret_p), ret_s)
```

```python
import functools

import jax
import jax.numpy as jnp
from jax import lax
from jax.experimental import pallas as pl
from jax.experimental.pallas import tpu as pltpu

F32 = jnp.float32
BF16 = jnp.bfloat16

EPS = 1e-6
ROPE_THETA = 10000.0
WINDOWS = (128, 512, 2048)
DILATIONS = (1, 4, 16)
N_GROUPS = 3
A_HEADS = 8
A_HEAD_DIM = 64
A_WIDTH = A_HEADS * A_HEAD_DIM
A_BLOCK = 128
R_HEADS = 4
R_CHUNK = 256
PAST_LEN = 2048
NEG = -1e30

VMEM_LIMIT_BYTES = 56 * 1024 * 1024
LANES = 128
MXU_DIM = 256
ROW_TILE = 512
A_OUT_TILE = 256
DECODE_SEQS_PER_STEP = (4, 2, 1)
RETENTION_SEQS_PER_STEP = 2


def _cparams(*sem):
    return pltpu.CompilerParams(dimension_semantics=sem, vmem_limit_bytes=VMEM_LIMIT_BYTES)


def _resident(shape, index_map):
    return pl.BlockSpec(shape, index_map, pipeline_mode=pl.Buffered(1))


def _rms(x, g):
    ms = jnp.mean(x * x, axis=-1, keepdims=True)
    return x * lax.rsqrt(ms + EPS) * g


def _silu(x):
    return x * (1.0 / (1.0 + jnp.exp(-x)))


def _dot(a, b):
    return jnp.dot(a, b, preferred_element_type=F32)


def _dot_nt(a, b):
    return lax.dot_general(a, b, (((1,), (1,)), ((), ())), preferred_element_type=F32)


def _dot_tn(a, b):
    return lax.dot_general(a, b, (((0,), (0,)), ((), ())), preferred_element_type=F32)


def _a_proj_kernel(x_ref, gmix_ref, w_ref, gq_ref, gk_ref, cos_ref, sin_ref, ones_ref, *refs,
                   dils, tail_blocks, tiles_per_seq, transposed_q):
    n_qkv = N_GROUPS if transposed_q else 3 * N_GROUPS
    qkv_refs, kvt_refs, deint = refs[:n_qkv], refs[n_qkv:n_qkv + N_GROUPS], refs[n_qkv + N_GROUPS]
    tm = x_ref.shape[0]
    tile_in_seq = pl.program_id(0) % tiles_per_seq
    h = _rms(x_ref[...], gmix_ref[...]).astype(BF16)
    cos = jnp.concatenate([cos_ref[...]] * (A_WIDTH // LANES), axis=1)
    sin = jnp.concatenate([sin_ref[...]] * (A_WIDTH // LANES), axis=1)
    lane = lax.broadcasted_iota(jnp.int32, (1, A_WIDTH), 1)
    first_half = (lane % A_HEAD_DIM) < (A_HEAD_DIM // 2)
    ones = ones_ref[...]

    def normed_rope(p, gain):
        y = (p * p).astype(BF16)
        ss = jnp.concatenate([_dot(y[:, :MXU_DIM], ones), _dot(y[:, MXU_DIM:], ones)], axis=1)
        pn = p * lax.rsqrt(ss * (1.0 / A_HEAD_DIM) + EPS) * gain
        rot = jnp.where(first_half,
                        pltpu.roll(pn, A_WIDTH - A_HEAD_DIM // 2, 1),
                        pltpu.roll(pn, A_HEAD_DIM // 2, 1))
        return pn * cos + rot * sin

    def store_dilated(out_ref, slot, val, dil):
        if dil == 1:
            out_ref[...] = val.astype(BF16)
            return
        tiles = A_WIDTH // LANES
        for c in range(tiles):
            deint[slot * tiles + c] = val[:, c * LANES:(c + 1) * LANES]
        for r in range(dil):
            for c in range(tiles):
                out_ref[:, r * A_WIDTH + c * LANES:r * A_WIDTH + (c + 1) * LANES] = (
                    deint[slot * tiles + c, pl.ds(r, tm // dil, stride=dil), :].astype(BF16))

    for g in range(N_GROUPS):
        c0 = 3 * g * A_WIDTH
        q = normed_rope(_dot(h, w_ref[:, c0:c0 + A_WIDTH]), gq_ref[g]) * (A_HEAD_DIM ** -0.5)
        k = normed_rope(_dot(h, w_ref[:, c0 + A_WIDTH:c0 + 2 * A_WIDTH]), gk_ref[g])
        v = _dot(h, w_ref[:, c0 + 2 * A_WIDTH:c0 + 3 * A_WIDTH])
        if transposed_q:
            qkv_refs[g][...] = q.T.astype(BF16)
        else:
            store_dilated(qkv_refs[3 * g], 0, q, dils[g])
            store_dilated(qkv_refs[3 * g + 1], 1, k, dils[g])
            store_dilated(qkv_refs[3 * g + 2], 2, v, dils[g])

        kvt_ref = kvt_refs[g]
        tw = kvt_ref.shape[1]

        @pl.when(tile_in_seq >= tiles_per_seq - tail_blocks[g])
        def _(k=k, v=v, kvt_ref=kvt_ref, tw=tw):
            kvt_ref[0:A_WIDTH, :] = k[tm - tw:, :].T
            kvt_ref[A_WIDTH:, :] = v[tm - tw:, :].T


def _a_proj(x, gmix, w_all, layer, gq, gk, cos, sin, ones, *, tm, seq, keeps, dils, transposed_q,
            side=None):
    m, d = x.shape
    n_seq = m // seq
    tps = seq // tm
    ntab = cos.shape[0] // tm
    tws = [min(tm, kp) for kp in keeps]
    tail_blocks = tuple(kp // tw for kp, tw in zip(keeps, tws))

    def kvt_map(nb):
        return lambda i: (i // tps, 0, jnp.maximum(i % tps - (tps - nb), 0))

    if transposed_q:
        qkv_specs = [pl.BlockSpec((A_WIDTH, tm), lambda i: (0, i))] * N_GROUPS
        qkv_shapes = [jax.ShapeDtypeStruct((A_WIDTH, m), BF16)] * N_GROUPS
    else:
        qkv_specs, qkv_shapes = [], []
        for dil in dils:
            qkv_specs += [pl.BlockSpec((tm // dil, dil * A_WIDTH), lambda i: (i, 0))] * 3
            qkv_shapes += [jax.ShapeDtypeStruct((m // dil, dil * A_WIDTH), BF16)] * 3
    kvt_specs = [pl.BlockSpec((None, 2 * A_WIDTH, tw), kvt_map(nb)) for tw, nb in zip(tws, tail_blocks)]
    kvt_shapes = [jax.ShapeDtypeStruct((n_seq, 2 * A_WIDTH, kp), F32) for kp in keeps]
    gain_spec = pl.BlockSpec((N_GROUPS, 1, A_WIDTH), lambda i: (0, 0, 0))
    tab_spec = pl.BlockSpec((tm, LANES), lambda i: (i % ntab, 0))
    outs, rode = _call(
        functools.partial(_a_proj_kernel, dils=dils, tail_blocks=tail_blocks, tiles_per_seq=tps,
                          transposed_q=transposed_q),
        (m // tm,),
        [pl.BlockSpec((tm, d), lambda i: (i, 0)),
         pl.BlockSpec((1, d), lambda i: (0, 0)),
         _resident((None, d, 3 * N_GROUPS * A_WIDTH), lambda i: (layer, 0, 0)),
         gain_spec, gain_spec, tab_spec, tab_spec,
         pl.BlockSpec((MXU_DIM, MXU_DIM), lambda i: (0, 0))],
        qkv_specs + kvt_specs, qkv_shapes + kvt_shapes,
        [pltpu.VMEM((3 * A_WIDTH // LANES, tm, LANES), F32)],
        [x, gmix, w_all, gq, gk, cos, sin, ones], "a_proj", side)
    return outs[:-N_GROUPS], outs[-N_GROUPS:], rode


def _band_attn_kernel(q_ref, kp_ref, kc_ref, vp_ref, vc_ref, o_ref, lse_ref, kb, vb, *, nsub):
    i = pl.program_id(2)
    kb[0:A_BLOCK, :] = kp_ref[...]
    kb[A_BLOCK:, :] = kc_ref[...]
    vb[0:A_BLOCK, :] = vp_ref[...]
    vb[A_BLOCK:, :] = vc_ref[...]
    heads_per_half = MXU_DIM // A_HEAD_DIM
    stacked = (heads_per_half * A_BLOCK, 2 * A_BLOCK)
    qi = lax.broadcasted_iota(jnp.int32, stacked, 0) & (A_BLOCK - 1)
    kj = lax.broadcasted_iota(jnp.int32, stacked, 1)
    band = (kj >= qi) & (kj <= qi + A_BLOCK)
    band_first = band & ((kj >= A_BLOCK) | (i > 0))
    head_of_lane = lax.broadcasted_iota(jnp.int32, (1, MXU_DIM), 1) // A_HEAD_DIM

    for s in range(nsub):
        valid = band_first if s == 0 else band
        rows = slice(s * A_BLOCK, (s + 1) * A_BLOCK)
        keys = slice(s * A_BLOCK, (s + 2) * A_BLOCK)
        for half in range(A_WIDTH // MXU_DIM):
            lanes = slice(half * MXU_DIM, (half + 1) * MXU_DIM)
            qh = q_ref[rows, lanes]
            qs = jnp.concatenate(
                [jnp.where(head_of_lane == hd, qh, jnp.zeros_like(qh))
                 for hd in range(heads_per_half)], axis=0)
            sc = _dot_nt(qs, kb[keys, lanes])
            sc = jnp.where(valid, sc, NEG)
            m = jnp.max(sc, axis=1, keepdims=True)
            p = jnp.exp(sc - m)
            l = jnp.sum(p, axis=1, keepdims=True)
            r = _dot(p.astype(BF16), vb[keys, lanes]) * (1.0 / l)
            lse = m + jnp.log(l)
            o_half = jnp.zeros((A_BLOCK, MXU_DIM), F32)
            lse_half = jnp.zeros((A_BLOCK, MXU_DIM), F32)
            for hd in range(heads_per_half):
                hrows = slice(hd * A_BLOCK, (hd + 1) * A_BLOCK)
                sel = head_of_lane == hd
                o_half = jnp.where(sel, r[hrows, :], o_half)
                lse_half = jnp.where(sel, lse[hrows, :], lse_half)
            o_ref[rows, lanes] = o_half.astype(BF16)
            lse_ref[rows, lanes] = lse_half


def _band_attn_grid(g, batch, seq):
    dil = DILATIONS[g]
    sub = seq // dil
    tq = min(4 * A_BLOCK, sub)
    nb = sub // tq
    return (batch, dil, nb), (lambda b, r, i: (b * dil + r) * nb + i)


def _band_attn(q, k, v, g, batch, seq, side=None):
    grid, _ = _band_attn_grid(g, batch, seq)
    nb = grid[2]
    tq = seq // DILATIONS[g] // nb
    nsub = tq // A_BLOCK
    cur = pl.BlockSpec((tq, A_WIDTH), lambda b, r, i: (b * nb + i, r))
    prev = pl.BlockSpec((A_BLOCK, A_WIDTH),
                        lambda b, r, i: (b * nb * nsub + jnp.maximum(i * nsub - 1, 0), r))
    return _call(
        functools.partial(_band_attn_kernel, nsub=nsub), grid,
        [cur, prev, cur, prev, cur], [cur, cur],
        [jax.ShapeDtypeStruct(q.shape, BF16), jax.ShapeDtypeStruct(q.shape, F32)],
        [pltpu.VMEM((tq + A_BLOCK, A_WIDTH), BF16), pltpu.VMEM((tq + A_BLOCK, A_WIDTH), BF16)],
        [q, k, k, v, v], f"band_attn_g{g}", side)


def _a_out_kernel(o0_ref, o1_ref, o2_ref, l0_ref, l1_ref, l2_ref, x_ref, w_ref, out_ref, scr, *, dils):
    tm = x_ref.shape[0]

    def natural(ref, slot, dil):
        if dil == 1:
            return ref[...].astype(F32)
        tiles = A_WIDTH // LANES
        for r in range(dil):
            for c in range(tiles):
                scr[slot * tiles + c, pl.ds(r, tm // dil, stride=dil), :] = (
                    ref[:, r * A_WIDTH + c * LANES:r * A_WIDTH + (c + 1) * LANES].astype(F32))
        return jnp.concatenate([scr[slot * tiles + c] for c in range(tiles)], axis=1)

    o0, o1, o2 = (natural(r, s, d) for s, (r, d) in enumerate(zip((o0_ref, o1_ref, o2_ref), dils)))
    l0, l1, l2 = (natural(r, 3 + s, d) for s, (r, d) in enumerate(zip((l0_ref, l1_ref, l2_ref), dils)))
    mx = jnp.maximum(jnp.maximum(l0, l1), l2)
    w0, w1, w2 = jnp.exp(l0 - mx), jnp.exp(l1 - mx), jnp.exp(l2 - mx)
    o = (w0 * o0 + w1 * o1 + w2 * o2) / (w0 + w1 + w2)
    out_ref[...] = x_ref[...] + _dot(o.astype(BF16), w_ref[...])


def _a_out(os_, lses, x, w_all, layer, *, tm, dils, side=None):
    m, d = x.shape
    row = lambda i: (i, 0)
    grp = [pl.BlockSpec((tm // dil, dil * A_WIDTH), row) for dil in dils]
    return _call(
        functools.partial(_a_out_kernel, dils=dils), (m // tm,),
        grp + grp + [pl.BlockSpec((tm, d), row),
                     _resident((None, A_WIDTH, d), lambda i: (layer, 0, 0))],
        [pl.BlockSpec((tm, d), row)], [jax.ShapeDtypeStruct((m, d), F32)],
        [pltpu.VMEM((2 * N_GROUPS * A_WIDTH // LANES, tm, LANES), F32)],
        [*os_, *lses, x, w_all], "a_out", side)


def _split3(x):
    hi = x.astype(BF16)
    r1 = x - hi.astype(F32)
    mid = r1.astype(BF16)
    lo = (r1 - mid.astype(F32)).astype(BF16)
    return hi, mid, lo


def _decode_attn_body(qt_ref, newt_ref, c_ref, out_ref, ot_ref, lset_ref, first_seq, *, dil, n_rel):
    per_step, _, lg = c_ref.shape
    bd = qt_ref.shape[1]
    reps = lg // LANES
    lane_seq = lax.broadcasted_iota(jnp.int32, (1, bd), 1)
    row_seq = lax.broadcasted_iota(jnp.int32, (bd, LANES), 0)
    head_row = lax.broadcasted_iota(jnp.int32, (A_HEADS, 1), 0)
    pos = lax.broadcasted_iota(jnp.int32, (1, lg), 1)
    back = lg - pos
    valid = ((back & (dil - 1)) == 0) & (back <= n_rel * dil)
    keep = lax.broadcasted_iota(jnp.int32, (1, LANES), 1) < LANES - 1
    tiles = [slice(t * LANES, (t + 1) * LANES) for t in range(reps)]

    def stream(e, rows, use, newcol):
        prev = None
        for t in range(reps):
            x = c_ref[e, rows, tiles[t]]
            use(t, x)
            rot = pltpu.roll(x, LANES - 1, 1)
            if prev is not None:
                out_ref[e, rows, tiles[t - 1]] = jnp.where(keep, prev, rot)
            prev = rot
        out_ref[e, rows, tiles[reps - 1]] = jnp.where(keep, prev, newcol)

    def head_rows(col):
        return jnp.concatenate(
            [jnp.broadcast_to(col[hd:hd + 1, :], (A_HEAD_DIM, 1)) for hd in range(A_HEADS)], axis=0)

    krows = [slice(hd * A_HEAD_DIM, (hd + 1) * A_HEAD_DIM) for hd in range(A_HEADS)]
    vrows = [slice(A_WIDTH + hd * A_HEAD_DIM, A_WIDTH + (hd + 1) * A_HEAD_DIM) for hd in range(A_HEADS)]
    seqs = [first_seq + e for e in range(per_step)]
    qt = qt_ref[...]
    new_hi, new_mid, new_lo = _split3(newt_ref[...])
    qbs, newbs = [], []
    for b in seqs:
        pick = (row_seq == b).astype(BF16)
        qbs.append(_dot(qt, pick))
        newbs.append(_dot(new_hi, pick) + _dot(new_mid, pick) + _dot(new_lo, pick))

    scs, sc_news = [], []
    for e in range(per_step):
        sc_tiles = [jnp.zeros((A_HEADS, LANES), F32) for _ in range(reps)]
        sc_new = jnp.zeros((A_HEADS, LANES), F32)
        for hd in range(A_HEADS):
            qh = qbs[e][krows[hd], :]

            def score(t, kt, hd=hd, qh=qh):
                sc_tiles[t] = jnp.where(head_row == hd,
                                        jnp.sum(kt * qh, axis=0, keepdims=True), sc_tiles[t])

            stream(e, krows[hd], score, newbs[e][krows[hd], :])
            sc_new = jnp.where(head_row == hd,
                               jnp.sum(newbs[e][krows[hd], :] * qh, axis=0, keepdims=True), sc_new)
        sc = sc_tiles[0] if reps == 1 else jnp.concatenate(sc_tiles, axis=1)
        scs.append(jnp.where(valid, sc, NEG))
        sc_news.append(sc_new[:, 0:1])

    ps, p_news, ls, ms = [], [], [], []
    for e in range(per_step):
        m = jnp.maximum(jnp.max(scs[e], axis=1, keepdims=True), sc_news[e])
        p = jnp.where(valid, jnp.exp(scs[e] - m), 0.0)
        p_new = jnp.exp(sc_news[e] - m)
        ps.append(p)
        p_news.append(p_new)
        ms.append(m)
        ls.append(jnp.sum(p, axis=1, keepdims=True) + p_new)

    for e in range(per_step):
        parts = []
        for hd in range(A_HEADS):
            part = [jnp.zeros((A_HEAD_DIM, LANES), F32)]

            def weigh(t, vt, hd=hd, part=part, e=e):
                part[0] = part[0] + vt * ps[e][hd:hd + 1, tiles[t]]

            stream(e, vrows[hd], weigh, newbs[e][vrows[hd], :])
            parts.append(part[0])
        acc = jnp.sum(jnp.concatenate(parts, axis=0), axis=1, keepdims=True)
        o = (acc + head_rows(p_news[e]) * newbs[e][A_WIDTH:, 0:1]) / head_rows(ls[e])
        onehot = lane_seq == seqs[e]
        ot_ref[...] = jnp.where(onehot, o, ot_ref[...])
        lset_ref[...] = jnp.where(onehot, ms[e] + jnp.log(ls[e]), lset_ref[...])


class _DecodeSide:
    def __init__(self, qt, newt, cache, layer, g, carry, first_seq, count, grid, lin):
        nl, bd, rows, lg = cache.shape
        per_step = DECODE_SEQS_PER_STEP[g]
        assert count % per_step == 0 and first_seq % per_step == 0
        n = count // per_step
        host_steps = 1
        for s in grid:
            host_steps *= s
        assert n <= host_steps
        kv_prev, ot_prev, lse_prev = carry
        const = lambda *idx: (0, 0)
        buf_spec = pl.BlockSpec(
            (None, per_step, rows, lg),
            lambda *idx: (layer, first_seq // per_step + jnp.minimum(lin(*idx), n - 1), 0, 0))
        ot_spec = pl.BlockSpec((A_WIDTH, bd), const)
        lse_spec = pl.BlockSpec((A_HEADS, bd), const)
        self.in_specs = [pl.BlockSpec((A_WIDTH, bd), const), pl.BlockSpec((rows, bd), const),
                         buf_spec, ot_spec, lse_spec]
        self.args = [qt, newt, cache, ot_prev, lse_prev]
        self.aliases = {}
        if kv_prev is not None:
            self.in_specs.append(pl.BlockSpec(memory_space=pl.ANY))
            self.args.append(kv_prev)
            self.aliases = {5: 0}
        self.out_specs = [buf_spec, ot_spec, lse_spec]
        self.out_shape = [jax.ShapeDtypeStruct(cache.shape, F32),
                          jax.ShapeDtypeStruct((A_WIDTH, bd), F32),
                          jax.ShapeDtypeStruct((A_HEADS, bd), F32)]

        self.scratch_shapes = []

        def body(ins, outs, scratch):
            qt_ref, newt_ref, c_ref, ot_in, lse_in = ins[:5]
            out_ref, ot_ref, lset_ref = outs
            step = lin(*[pl.program_id(a) for a in range(len(grid))])

            @pl.when(step == 0)
            def _():
                ot_ref[...] = ot_in[...]
                lset_ref[...] = lse_in[...]

            def work():
                _decode_attn_body(qt_ref, newt_ref, c_ref, out_ref, ot_ref, lset_ref,
                                  first_seq + step * per_step,
                                  dil=DILATIONS[g], n_rel=WINDOWS[g] // DILATIONS[g])

            if n < host_steps:
                pl.when(step < n)(work)
            else:
                work()

        self.body = body


def _call(host_kernel, grid, in_specs, out_specs, out_shape, scratch_shapes, args, name, side=None):
    n_in, n_out = len(in_specs), len(out_specs)
    if side is None:
        kernel, aliases, side_args = host_kernel, {}, []
    else:
        n_sin, n_sout = len(side.in_specs), len(side.out_specs)
        n_scr = len(scratch_shapes)

        def kernel(*refs):
            o0 = n_in + n_sin
            s0 = o0 + n_out + n_sout
            if host_kernel is not None:
                host_kernel(*refs[:n_in], *refs[o0:o0 + n_out], *refs[s0:s0 + n_scr])
            side.body(refs[n_in:o0], refs[o0 + n_out:s0], refs[s0 + n_scr:])

        aliases = {n_in + a: n_out + b for a, b in side.aliases.items()}
        in_specs, side_args = in_specs + side.in_specs, side.args
        out_specs, out_shape = out_specs + side.out_specs, out_shape + side.out_shape
        scratch_shapes = scratch_shapes + side.scratch_shapes
    outs = pl.pallas_call(
        kernel, grid=grid, in_specs=in_specs, out_specs=out_specs, out_shape=out_shape,
        scratch_shapes=scratch_shapes, input_output_aliases=aliases,
        compiler_params=_cparams(*(("arbitrary",) * len(grid))), name=name,
    )(*args, *side_args)
    return outs[:n_out], outs[n_out:]


class _Rider:
    def __init__(self, total, per_step, make, carry, name):
        self.total, self.per_step, self.make, self.carry, self.name = total, per_step, make, carry, name
        self.done = 0
        self.pending = False

    def ride(self, grid, lin=lambda i: i):
        steps = 1
        for s in grid:
            steps *= s
        count = min(steps * self.per_step, self.total - self.done)
        self.pending = count > 0
        if not self.pending:
            return None
        side = self.make(self.carry, self.done, count, grid, lin)
        self.done += count
        return side

    def update(self, rode):
        if self.pending:
            self.carry = tuple(rode)
            self.pending = False

    def finish(self):
        if self.done < self.total:
            grid = ((self.total - self.done) // self.per_step,)
            _, rode = _call(None, grid, [], [], [], [], [], self.name, self.ride(grid))
            self.update(rode)
        return self.carry


def _ffn_kernel(x_ref, g_ref, wg_ref, wu_ref, wd_ref, out_ref, *, chunk):
    x = x_ref[...]
    h = _rms(x, g_ref[...]).astype(BF16)
    out_ref[...] = x
    for c in range(wg_ref.shape[1] // chunk):
        cols = slice(c * chunk, (c + 1) * chunk)
        gate = _dot(h, wg_ref[:, cols])
        up = _dot(h, wu_ref[:, cols])
        out_ref[...] += _dot((_silu(gate) * up).astype(BF16), wd_ref[cols, :])


def _ffn(x, g, wg_all, wu_all, wd_all, layer, *, tm, side=None):
    m, d = x.shape
    f = wg_all.shape[2]
    row = lambda i: (i, 0)
    sel = lambda i: (layer, 0, 0)
    (out,), rode = _call(
        functools.partial(_ffn_kernel, chunk=MXU_DIM), (m // tm,),
        [pl.BlockSpec((tm, d), row), pl.BlockSpec((1, d), lambda i: (0, 0)),
         _resident((None, d, f), sel), _resident((None, d, f), sel), _resident((None, f, d), sel)],
        [pl.BlockSpec((tm, d), row)], [jax.ShapeDtypeStruct((m, d), F32)], [],
        [x, g, wg_all, wu_all, wd_all], "ffn", side)
    return out, rode


def _r_proj_kernel(x_ref, gmix_ref, w_ref, cos_ref, sin_ref, out_ref, *qkt_ref, qk_dim):
    h = _rms(x_ref[...], gmix_ref[...]).astype(BF16)
    half = qk_dim // 2
    nqk = 2 * R_HEADS * qk_dim
    cos, sin = cos_ref[...], sin_ref[...]
    p = _dot(h, w_ref[:, 0:nqk])
    for hd in range(2 * R_HEADS):
        scale = 1.0 if hd < R_HEADS else qk_dim ** -0.5
        c0 = hd * qk_dim
        x1, x2 = p[:, c0:c0 + half], p[:, c0 + half:c0 + qk_dim]
        o1 = ((x1 * cos - x2 * sin) * scale).astype(BF16)
        o2 = ((x2 * cos + x1 * sin) * scale).astype(BF16)
        out_ref[:, c0:c0 + half] = o1
        out_ref[:, c0 + half:c0 + qk_dim] = o2
        if qkt_ref:
            qkt_ref[0][c0:c0 + half, :] = o1.astype(F32).T.astype(BF16)
            qkt_ref[0][c0 + half:c0 + qk_dim, :] = o2.astype(F32).T.astype(BF16)
    for c0 in range(nqk, w_ref.shape[1], nqk):
        out_ref[:, c0:c0 + nqk] = _dot(h, w_ref[:, c0:c0 + nqk]).astype(BF16)


def _r_proj(x, gmix, w_all, layer, cos, sin, *, tm, transposed_qk=False):
    m, d = x.shape
    n = w_all.shape[2]
    qk_dim = n // (6 * R_HEADS)
    nqk = 2 * R_HEADS * qk_dim
    ntab = cos.shape[0] // tm
    tab = pl.BlockSpec((tm, qk_dim // 2), lambda i: (i % ntab, 0))
    out_specs = [pl.BlockSpec((tm, n), lambda i: (i, 0))]
    out_shape = [jax.ShapeDtypeStruct((m, n), BF16)]
    if transposed_qk:
        out_specs.append(pl.BlockSpec((nqk, tm), lambda i: (0, i)))
        out_shape.append(jax.ShapeDtypeStruct((nqk, m), BF16))
    return pl.pallas_call(
        functools.partial(_r_proj_kernel, qk_dim=qk_dim),
        grid=(m // tm,),
        in_specs=[pl.BlockSpec((tm, d), lambda i: (i, 0)),
                  pl.BlockSpec((1, d), lambda i: (0, 0)),
                  _resident((None, d, n), lambda i: (layer, 0, 0)), tab, tab],
        out_specs=out_specs,
        out_shape=out_shape,
        compiler_params=_cparams("arbitrary"),
        name="r_proj",
    )(x, gmix, w_all, cos, sin)


def _retention_kernel(p_ref, x_ref, dmask_ref, qdec_ref, kdec_ref, cdec_ref, gout_ref, wo_ref,
                      out_ref, state_ref, y_scr, *, qk, dv):
    c = pl.program_id(0)

    @pl.when(c == 0)
    def _():
        state_ref[...] = jnp.zeros_like(state_ref)

    v0 = 2 * R_HEADS * qk
    g0 = v0 + R_HEADS * dv
    for b in range(p_ref.shape[0]):
        for hd in range(R_HEADS):
            q = p_ref[b, :, hd * qk:(hd + 1) * qk]
            k = p_ref[b, :, (R_HEADS + hd) * qk:(R_HEADS + hd + 1) * qk]
            v = p_ref[b, :, v0 + hd * dv:v0 + (hd + 1) * dv]
            gate = p_ref[b, :, g0 + hd * dv:g0 + (hd + 1) * dv].astype(F32)
            s_prev = state_ref[b, hd]
            inter = _dot(q, s_prev.astype(BF16)) * qdec_ref[hd]
            scores = _dot_nt(q, k) * dmask_ref[hd]
            o = inter + _dot(scores.astype(BF16), v)
            kd = (k.astype(F32) * kdec_ref[hd]).astype(BF16)
            state_ref[b, hd] = s_prev * cdec_ref[hd][:, 0:1] + _dot_tn(kd, v)
            y = _rms(o, gout_ref[hd]) * _silu(gate)
            y_scr[b, :, hd * dv:(hd + 1) * dv] = y.astype(BF16)
        out_ref[b] = x_ref[b] + _dot(y_scr[b], wo_ref[...])


def _retention_tables(chunk):
    lg = jnp.log1p(-jnp.exp2(-5.0 - jnp.arange(R_HEADS, dtype=F32)))
    idx = jnp.arange(chunk, dtype=F32)
    rel = idx[:, None] - idx[None, :]
    dmask = jnp.where(rel >= 0, jnp.exp(jnp.maximum(rel, 0.0) * lg[:, None, None]), 0.0)
    qdec = jnp.exp((idx + 1.0)[None, :] * lg[:, None])[:, :, None]
    kdec = jnp.exp((chunk - 1.0 - idx)[None, :] * lg[:, None])[:, :, None]
    cdec = jnp.broadcast_to(jnp.exp(chunk * lg)[:, None, None], (R_HEADS, 1, LANES))
    return dmask, qdec, kdec, cdec


def _retention(p, x, gout, wo_all, layer, batch, seq, *, chunk, side=None):
    m, n = p.shape
    d = x.shape[1]
    qk = n // (6 * R_HEADS)
    dv = 2 * qk
    tables = _retention_tables(chunk)
    full = lambda a: pl.BlockSpec(a.shape, lambda c: (0,) * a.ndim)
    rows = lambda width: pl.BlockSpec((batch, chunk, width), lambda c: (0, c, 0))
    (out, state), rode = _call(
        functools.partial(_retention_kernel, qk=qk, dv=dv), (seq // chunk,),
        [rows(n), rows(d)] + [full(t) for t in tables] + [
            full(gout), _resident((None, R_HEADS * dv, d), lambda c: (layer, 0, 0))],
        [rows(d), pl.BlockSpec((batch, R_HEADS, qk, dv), lambda c: (0, 0, 0, 0))],
        [jax.ShapeDtypeStruct((batch, seq, d), F32),
         jax.ShapeDtypeStruct((batch, R_HEADS, qk, dv), F32)],
        [pltpu.VMEM((batch, chunk, R_HEADS * dv), BF16)],
        [p.reshape(batch, seq, n), x.reshape(batch, seq, d), *tables, gout, wo_all],
        "retention", side)
    return out.reshape(m, d), state, rode


def _retention_step_body(p_ref, qkt_ref, s_ref, gout_ref, y_ref, snew_ref, first_seq, *, qk, dv):
    per_step = s_ref.shape[0]
    bd = qkt_ref.shape[1]
    row_seq = lax.broadcasted_iota(jnp.int32, (bd, LANES), 0)
    qkt = qkt_ref[...]
    v0 = 2 * R_HEADS * qk
    g0 = v0 + R_HEADS * dv
    for e in range(per_step):
        pick = (row_seq == first_seq + e).astype(BF16)
        qkb = _dot(qkt, pick)
        for hd in range(R_HEADS):
            gamma = 1.0 - 2.0 ** (-5 - hd)
            qb = qkb[hd * qk:(hd + 1) * qk, :]
            kb = qkb[(R_HEADS + hd) * qk:(R_HEADS + hd + 1) * qk, :]
            qk_dot = jnp.sum(qb * kb, axis=0, keepdims=True)
            o_tiles = []
            for t in range(dv // LANES):
                lanes = slice(t * LANES, (t + 1) * LANES)
                v = p_ref[e, :, v0 + hd * dv + t * LANES:v0 + hd * dv + (t + 1) * LANES].astype(F32)
                s_prev = s_ref[e, hd, :, lanes]
                inter = jnp.sum(s_prev * qb, axis=0, keepdims=True) * gamma
                o_tiles.append(inter + qk_dot * v)
                snew_ref[e, hd, :, lanes] = s_prev * gamma + kb * v
            o = jnp.concatenate(o_tiles, axis=1)
            gate = p_ref[e, :, g0 + hd * dv:g0 + (hd + 1) * dv].astype(F32)
            y = _rms(o, gout_ref[hd]) * _silu(gate)
            y_ref[e, :, hd * dv:(hd + 1) * dv] = y.astype(BF16)


class _RetentionSide:
    def __init__(self, p, qkt, state, layer, gout, carry, first_seq, count, grid, lin):
        bd = p.shape[0]
        nl, _, nh, qk, dv = state.shape
        per_step = RETENTION_SEQS_PER_STEP
        assert count % per_step == 0 and first_seq % per_step == 0
        n = count // per_step
        host_steps = 1
        for s in grid:
            host_steps *= s
        assert n <= host_steps
        blk = lambda *idx: first_seq // per_step + jnp.minimum(lin(*idx), n - 1)
        state_spec = pl.BlockSpec((None, per_step, nh, qk, dv),
                                  lambda *idx: (layer, blk(*idx), 0, 0, 0))
        y_spec = pl.BlockSpec((per_step, 1, nh * dv), lambda *idx: (blk(*idx), 0, 0))
        self.in_specs = [pl.BlockSpec((per_step, 1, p.shape[2]), lambda *idx: (blk(*idx), 0, 0)),
                         pl.BlockSpec(qkt.shape, lambda *idx: (0, 0)),
                         state_spec,
                         pl.BlockSpec((nh, 1, dv), lambda *idx: (0, 0, 0))]
        self.args = [p, qkt, state, gout]
        self.aliases = {}
        for out_idx, prev in enumerate(carry):
            if prev is not None:
                self.aliases[len(self.args)] = out_idx
                self.in_specs.append(pl.BlockSpec(memory_space=pl.ANY))
                self.args.append(prev)
        self.out_specs = [y_spec, state_spec]
        self.out_shape = [jax.ShapeDtypeStruct((bd, 1, nh * dv), BF16),
                          jax.ShapeDtypeStruct(state.shape, F32)]

        self.scratch_shapes = []

        def body(ins, outs, scratch):
            step = lin(*[pl.program_id(a) for a in range(len(grid))])

            def work():
                _retention_step_body(*ins[:4], *outs, first_seq + step * per_step, qk=qk, dv=dv)

            if n < host_steps:
                pl.when(step < n)(work)
            else:
                work()

        self.body = body


def _proj_res_kernel(y_ref, x_ref, w_ref, out_ref):
    out_ref[...] = x_ref[...] + _dot(y_ref[...], w_ref[...])


def _proj_res(y, x, w_all, layer, *, tm):
    m, d = x.shape
    kdim = y.shape[1]
    row = lambda i: (i, 0)
    return pl.pallas_call(
        _proj_res_kernel,
        grid=(m // tm,),
        in_specs=[pl.BlockSpec((tm, kdim), row), pl.BlockSpec((tm, d), row),
                  _resident((None, kdim, d), lambda i: (layer, 0, 0))],
        out_specs=pl.BlockSpec((tm, d), row),
        out_shape=jax.ShapeDtypeStruct((m, d), F32),
        compiler_params=_cparams("arbitrary"),
        name="proj_res",
    )(y, x, w_all)


def _rope_tables(pos, half, width, sign_fold):
    lane = jnp.arange(width)
    inv = ROPE_THETA ** (-(lane % half).astype(F32) / half)
    ang = pos.astype(F32)[:, None] * inv[None, :]
    cos, sin = jnp.cos(ang), jnp.sin(ang)
    if sign_fold:
        sin = jnp.where((lane % (2 * half)) < half, -sin, sin)
    return cos, sin


def kernel(x_prompt, x_sample, cache_kv_g0, cache_kv_g1, cache_kv_g2, state_retention,
           mix_norm, ffn_norm, a_w_qkv, a_q_norm, a_k_norm, a_w_o,
           r_w_in, r_out_norm, r_w_o, f_w_gate, f_w_up, f_w_down):
    batch, seq, d = x_prompt.shape
    bd, dec_seq, _ = x_sample.shape
    assert dec_seq == 1
    depth = mix_norm.shape[0]
    caches = (cache_kv_g0, cache_kv_g1, cache_kv_g2)
    keeps = tuple(min(w, seq) for w in WINDOWS)
    tm = ROW_TILE
    qk_dim = r_w_in.shape[2] // (6 * R_HEADS)

    xp = x_prompt.reshape(batch * seq, d)
    xs = x_sample.reshape(bd, d)

    caches_t = [jnp.transpose(c, (0, 1, 3, 4, 5, 2)).reshape(c.shape[0], bd, 2 * A_WIDTH, c.shape[2])
                for c in caches]

    a_w_qkv, a_w_o, r_w_in, r_w_o, f_w_gate, f_w_up, f_w_down = (
        w.astype(BF16) for w in (a_w_qkv, a_w_o, r_w_in, r_w_o, f_w_gate, f_w_up, f_w_down))

    pos_p = jnp.arange(seq)
    pos_s = jnp.full((bd,), PAST_LEN)
    a_tabs_p = _rope_tables(pos_p, A_HEAD_DIM // 2, LANES, True)
    a_tabs_s = _rope_tables(pos_s, A_HEAD_DIM // 2, LANES, True)
    r_tabs_p = _rope_tables(pos_p, qk_dim // 2, qk_dim // 2, False)
    r_tabs_s = _rope_tables(pos_s, qk_dim // 2, qk_dim // 2, False)
    lane = jnp.arange(MXU_DIM)
    ones_bd = (lane[:, None] // A_HEAD_DIM == lane[None, :] // A_HEAD_DIM).astype(BF16)
    no_dil = (1,) * N_GROUPS

    kv_p = [[] for _ in range(N_GROUPS)]
    kv_s = [None] * N_GROUPS
    ret_p, ret_s = [], None
    for i in range(depth):
        j = i // 2
        gmix = mix_norm[i][None, :]
        if i % 2 == 0:
            gq = jnp.tile(a_q_norm[j], (1, A_HEADS))[:, None, :]
            gk = jnp.tile(a_k_norm[j], (1, A_HEADS))[:, None, :]
            qts, newts, _ = _a_proj(xs, gmix, a_w_qkv, j, gq, gk, *a_tabs_s, ones_bd, tm=bd, seq=bd,
                                    keeps=(bd,) * N_GROUPS, dils=no_dil, transposed_q=True)
            riders = [
                _Rider(bd, DECODE_SEQS_PER_STEP[g],
                       functools.partial(_DecodeSide, qts[g], newts[g][0], caches_t[g], j, g),
                       (kv_s[g], jnp.zeros((A_WIDTH, bd), F32), jnp.zeros((A_HEADS, bd), F32)),
                       f"decode_attn_g{g}")
                for g in range(N_GROUPS)]
            qkv, kvt, rode = _a_proj(xp, gmix, a_w_qkv, j, gq, gk, *a_tabs_p, ones_bd, tm=tm,
                                     seq=seq, keeps=keeps, dils=DILATIONS, transposed_q=False,
                                     side=riders[0].ride((batch * seq // tm,)))
            riders[0].update(rode)
            os_, lses = [], []
            for g in range(N_GROUPS):
                (o, lse), rode = _band_attn(*qkv[3 * g:3 * g + 3], g, batch, seq,
                                            riders[2].ride(*_band_attn_grid(g, batch, seq)))
                riders[2].update(rode)
                os_.append(o)
                lses.append(lse)
                kv_p[g].append(kvt[g])
            (xp,), rode = _a_out(os_, lses, xp, a_w_o, j, tm=A_OUT_TILE, dils=DILATIONS,
                                 side=riders[2].ride((batch * seq // A_OUT_TILE,)))
            riders[2].update(rode)
            ffn_rider = riders[1]

            def sample_mixer(xs):
                os_, lses = [], []
                for g in range(N_GROUPS):
                    kv_s[g], ot, lset = riders[g].finish()
                    os_.append(ot.T)
                    lses.append(jnp.repeat(lset.T, A_HEAD_DIM, axis=1))
                (xs,), _ = _a_out(os_, lses, xs, a_w_o, j, tm=bd, dils=no_dil)
                return xs
        else:
            gout = r_out_norm[j][:, None, :]
            ps, qkt = _r_proj(xs, gmix, r_w_in, j, *r_tabs_s, tm=bd, transposed_qk=True)
            ffn_rider = _Rider(
                bd, RETENTION_SEQS_PER_STEP,
                functools.partial(_RetentionSide, ps[:, None, :], qkt, state_retention, j, gout),
                (None, ret_s), "retention_step")
            p, = _r_proj(xp, gmix, r_w_in, j, *r_tabs_p, tm=tm)
            xp, s_fin, rode = _retention(p, xp, gout, r_w_o, j, batch, seq, chunk=R_CHUNK,
                                         side=ffn_rider.ride((seq // R_CHUNK,)))
            ffn_rider.update(rode)
            ret_p.append(s_fin)

            def sample_mixer(xs):
                nonlocal ret_s
                ys, ret_s = ffn_rider.finish()
                return _proj_res(ys.reshape(bd, -1), xs, r_w_o, j, tm=bd)
        gffn = ffn_norm[i][None, :]
        xp, rode = _ffn(xp, gffn, f_w_gate, f_w_up, f_w_down, i, tm=tm,
                        side=ffn_rider.ride((batch * seq // tm,)))
        ffn_rider.update(rode)
        xs = sample_mixer(xs)
        xs, _ = _ffn(xs, gffn, f_w_gate, f_w_up, f_w_down, i, tm=bd)

    def rows_major(t):
        lead, slots = t.shape[:-2], t.shape[-1]
        t = t.reshape(lead + (2, A_HEADS, A_HEAD_DIM, slots))
        return jnp.moveaxis(t, -1, len(lead))

    return (xp.reshape(batch, seq, d), xs.reshape(bd, 1, d),
            rows_major(jnp.stack(kv_p[0])), rows_major(kv_s[0]),
            rows_major(jnp.stack(kv_p[1])), rows_major(kv_s[1]),
            rows_major(jnp.stack(kv_p[2])), rows_major(kv_s[2]),
            jnp.stack(ret_p), ret_s)
```

```python
import functools

import jax
import jax.numpy as jnp
from jax import lax
from jax.experimental import pallas as pl
from jax.experimental.pallas import tpu as pltpu

F32 = jnp.float32
BF16 = jnp.bfloat16

EPS = 1e-6
ROPE_THETA = 10000.0
WINDOWS = (128, 512, 2048)
DILATIONS = (1, 4, 16)
N_GROUPS = 3
A_HEADS = 8
A_HEAD_DIM = 64
A_WIDTH = A_HEADS * A_HEAD_DIM
A_BLOCK = 128
R_HEADS = 4
R_CHUNK = 256
PAST_LEN = 2048
NEG = -1e30

VMEM_LIMIT_BYTES = 56 * 1024 * 1024
LANES = 128
MXU_DIM = 256
LSE_SEG = LANES // A_HEADS
ROW_TILE = 512
A_OUT_TILE = 256
DECODE_SEQS_PER_STEP = (4, 2, 1)
RETENTION_SEQS_PER_STEP = 2


def _cparams(*sem):
    return pltpu.CompilerParams(dimension_semantics=sem, vmem_limit_bytes=VMEM_LIMIT_BYTES)


def _resident(shape, index_map):
    return pl.BlockSpec(shape, index_map, pipeline_mode=pl.Buffered(1))


def _rms(x, g):
    ms = jnp.mean(x * x, axis=-1, keepdims=True)
    return x * lax.rsqrt(ms + EPS) * g


def _silu(x):
    return x * (1.0 / (1.0 + jnp.exp(-x)))


def _dot(a, b):
    return jnp.dot(a, b, preferred_element_type=F32)


def _dot_nt(a, b):
    return lax.dot_general(a, b, (((1,), (1,)), ((), ())), preferred_element_type=F32)


def _dot_tn(a, b):
    return lax.dot_general(a, b, (((0,), (0,)), ((), ())), preferred_element_type=F32)


def _a_proj_kernel(x_ref, gmix_ref, w_ref, gq_ref, gk_ref, cos_ref, sin_ref, ones_ref, *refs,
                   dils, tail_blocks, tiles_per_seq, transposed_q):
    n_qkv = N_GROUPS if transposed_q else 3 * N_GROUPS
    qkv_refs, kvt_refs, deint = refs[:n_qkv], refs[n_qkv:n_qkv + N_GROUPS], refs[n_qkv + N_GROUPS]
    tm = x_ref.shape[0]
    tile_in_seq = pl.program_id(0) % tiles_per_seq
    h = _rms(x_ref[...], gmix_ref[...]).astype(BF16)
    cos = jnp.concatenate([cos_ref[...]] * (A_WIDTH // LANES), axis=1)
    sin = jnp.concatenate([sin_ref[...]] * (A_WIDTH // LANES), axis=1)
    lane = lax.broadcasted_iota(jnp.int32, (1, A_WIDTH), 1)
    first_half = (lane % A_HEAD_DIM) < (A_HEAD_DIM // 2)
    ones = ones_ref[...]

    def normed_rope(p, gain):
        y = (p * p).astype(BF16)
        ss = jnp.concatenate([_dot(y[:, :MXU_DIM], ones), _dot(y[:, MXU_DIM:], ones)], axis=1)
        pn = p * lax.rsqrt(ss * (1.0 / A_HEAD_DIM) + EPS) * gain
        rot = jnp.where(first_half,
                        pltpu.roll(pn, A_WIDTH - A_HEAD_DIM // 2, 1),
                        pltpu.roll(pn, A_HEAD_DIM // 2, 1))
        return pn * cos + rot * sin

    def store_dilated(out_ref, slot, val, dil):
        if dil == 1:
            out_ref[...] = val.astype(BF16)
            return
        tiles = A_WIDTH // LANES
        for c in range(tiles):
            deint[slot * tiles + c] = val[:, c * LANES:(c + 1) * LANES]
        for r in range(dil):
            for c in range(tiles):
                out_ref[:, r * A_WIDTH + c * LANES:r * A_WIDTH + (c + 1) * LANES] = (
                    deint[slot * tiles + c, pl.ds(r, tm // dil, stride=dil), :].astype(BF16))

    for g in range(N_GROUPS):
        c0 = 3 * g * A_WIDTH
        q = normed_rope(_dot(h, w_ref[:, c0:c0 + A_WIDTH]), gq_ref[g]) * (A_HEAD_DIM ** -0.5)
        k = normed_rope(_dot(h, w_ref[:, c0 + A_WIDTH:c0 + 2 * A_WIDTH]), gk_ref[g])
        v = _dot(h, w_ref[:, c0 + 2 * A_WIDTH:c0 + 3 * A_WIDTH])
        if transposed_q:
            qkv_refs[g][...] = q.T.astype(BF16)
        else:
            store_dilated(qkv_refs[3 * g], 0, q, dils[g])
            store_dilated(qkv_refs[3 * g + 1], 1, k, dils[g])
            store_dilated(qkv_refs[3 * g + 2], 2, v, dils[g])

        kvt_ref = kvt_refs[g]
        tw = kvt_ref.shape[1]

        @pl.when(tile_in_seq >= tiles_per_seq - tail_blocks[g])
        def _(k=k, v=v, kvt_ref=kvt_ref, tw=tw):
            kvt_ref[0:A_WIDTH, :] = k[tm - tw:, :].T
            kvt_ref[A_WIDTH:, :] = v[tm - tw:, :].T


def _a_proj(x, gmix, w_all, layer, gq, gk, cos, sin, ones, *, tm, seq, keeps, dils, transposed_q,
            side=None):
    m, d = x.shape
    n_seq = m // seq
    tps = seq // tm
    ntab = cos.shape[0] // tm
    tws = [min(tm, kp) for kp in keeps]
    tail_blocks = tuple(kp // tw for kp, tw in zip(keeps, tws))

    def kvt_map(nb):
        return lambda i: (i // tps, 0, jnp.maximum(i % tps - (tps - nb), 0))

    if transposed_q:
        qkv_specs = [pl.BlockSpec((A_WIDTH, tm), lambda i: (0, i))] * N_GROUPS
        qkv_shapes = [jax.ShapeDtypeStruct((A_WIDTH, m), BF16)] * N_GROUPS
    else:
        qkv_specs, qkv_shapes = [], []
        for dil in dils:
            qkv_specs += [pl.BlockSpec((tm // dil, dil * A_WIDTH), lambda i: (i, 0))] * 3
            qkv_shapes += [jax.ShapeDtypeStruct((m // dil, dil * A_WIDTH), BF16)] * 3
    kvt_specs = [pl.BlockSpec((None, 2 * A_WIDTH, tw), kvt_map(nb)) for tw, nb in zip(tws, tail_blocks)]
    kvt_shapes = [jax.ShapeDtypeStruct((n_seq, 2 * A_WIDTH, kp), F32) for kp in keeps]
    gain_spec = pl.BlockSpec((N_GROUPS, 1, A_WIDTH), lambda i: (0, 0, 0))
    tab_spec = pl.BlockSpec((tm, LANES), lambda i: (i % ntab, 0))
    outs, rode = _call(
        functools.partial(_a_proj_kernel, dils=dils, tail_blocks=tail_blocks, tiles_per_seq=tps,
                          transposed_q=transposed_q),
        (m // tm,),
        [pl.BlockSpec((tm, d), lambda i: (i, 0)),
         pl.BlockSpec((1, d), lambda i: (0, 0)),
         _resident((None, d, 3 * N_GROUPS * A_WIDTH), lambda i: (layer, 0, 0)),
         gain_spec, gain_spec, tab_spec, tab_spec,
         pl.BlockSpec((MXU_DIM, MXU_DIM), lambda i: (0, 0))],
        qkv_specs + kvt_specs, qkv_shapes + kvt_shapes,
        [pltpu.VMEM((3 * A_WIDTH // LANES, tm, LANES), F32)],
        [x, gmix, w_all, gq, gk, cos, sin, ones], "a_proj", side)
    return outs[:-N_GROUPS], outs[-N_GROUPS:], rode


def _band_attn_kernel(q_ref, kp_ref, kc_ref, vp_ref, vc_ref, o_ref, lse_ref, kb, vb, *, nsub):
    i = pl.program_id(2)
    kb[0:A_BLOCK, :] = kp_ref[...]
    kb[A_BLOCK:, :] = kc_ref[...]
    vb[0:A_BLOCK, :] = vp_ref[...]
    vb[A_BLOCK:, :] = vc_ref[...]
    heads_per_half = MXU_DIM // A_HEAD_DIM
    stacked = (heads_per_half * A_BLOCK, 2 * A_BLOCK)
    qi = lax.broadcasted_iota(jnp.int32, stacked, 0) & (A_BLOCK - 1)
    kj = lax.broadcasted_iota(jnp.int32, stacked, 1)
    band = (kj >= qi) & (kj <= qi + A_BLOCK)
    band_first = band & ((kj >= A_BLOCK) | (i > 0))
    head_of_lane = lax.broadcasted_iota(jnp.int32, (1, MXU_DIM), 1) // A_HEAD_DIM

    seg_of_lane = lax.broadcasted_iota(jnp.int32, (1, LANES), 1) // LSE_SEG

    for s in range(nsub):
        valid = band_first if s == 0 else band
        rows = slice(s * A_BLOCK, (s + 1) * A_BLOCK)
        keys = slice(s * A_BLOCK, (s + 2) * A_BLOCK)
        lse_all = jnp.zeros((A_BLOCK, LANES), F32)
        for half in range(A_WIDTH // MXU_DIM):
            lanes = slice(half * MXU_DIM, (half + 1) * MXU_DIM)
            qh = q_ref[rows, lanes]
            qs = jnp.concatenate(
                [jnp.where(head_of_lane == hd, qh, jnp.zeros_like(qh))
                 for hd in range(heads_per_half)], axis=0)
            sc = _dot_nt(qs, kb[keys, lanes])
            sc = jnp.where(valid, sc, NEG)
            m = jnp.max(sc, axis=1, keepdims=True)
            p = jnp.exp(sc - m)
            l = jnp.sum(p, axis=1, keepdims=True)
            r = _dot(p.astype(BF16), vb[keys, lanes]) * (1.0 / l)
            lse = m + jnp.log(l)
            o_half = jnp.zeros((A_BLOCK, MXU_DIM), F32)
            for hd in range(heads_per_half):
                hrows = slice(hd * A_BLOCK, (hd + 1) * A_BLOCK)
                o_half = jnp.where(head_of_lane == hd, r[hrows, :], o_half)
                lse_all = jnp.where(seg_of_lane == half * heads_per_half + hd, lse[hrows, :], lse_all)
            o_ref[rows, lanes] = o_half.astype(BF16)
        lse_ref[rows, :] = lse_all


def _band_attn_grid(g, batch, seq):
    dil = DILATIONS[g]
    sub = seq // dil
    tq = min(4 * A_BLOCK, sub)
    nb = sub // tq
    return (batch, dil, nb), (lambda b, r, i: (b * dil + r) * nb + i)


def _band_attn(q, k, v, g, batch, seq, side=None):
    grid, _ = _band_attn_grid(g, batch, seq)
    nb = grid[2]
    tq = seq // DILATIONS[g] // nb
    nsub = tq // A_BLOCK
    cur = pl.BlockSpec((tq, A_WIDTH), lambda b, r, i: (b * nb + i, r))
    prev = pl.BlockSpec((A_BLOCK, A_WIDTH),
                        lambda b, r, i: (b * nb * nsub + jnp.maximum(i * nsub - 1, 0), r))
    lse_spec = pl.BlockSpec((tq, LANES), lambda b, r, i: (b * nb + i, r))
    return _call(
        functools.partial(_band_attn_kernel, nsub=nsub), grid,
        [cur, prev, cur, prev, cur], [cur, lse_spec],
        [jax.ShapeDtypeStruct(q.shape, BF16),
         jax.ShapeDtypeStruct((q.shape[0], DILATIONS[g] * LANES), F32)],
        [pltpu.VMEM((tq + A_BLOCK, A_WIDTH), BF16), pltpu.VMEM((tq + A_BLOCK, A_WIDTH), BF16)],
        [q, k, k, v, v], f"band_attn_g{g}", side)


def _a_out_kernel(o0_ref, o1_ref, o2_ref, l0_ref, l1_ref, l2_ref, x_ref, w_ref, spread_ref,
                  out_ref, scr, *, dils):
    tm = x_ref.shape[0]
    o_tiles = A_WIDTH // LANES

    def natural(ref, slot, dil, tiles):
        if dil == 1:
            return ref[...].astype(F32)
        for r in range(dil):
            for c in range(tiles):
                scr[slot + c, pl.ds(r, tm // dil, stride=dil), :] = (
                    ref[:, (r * tiles + c) * LANES:(r * tiles + c + 1) * LANES].astype(F32))
        if tiles == 1:
            return scr[slot]
        return jnp.concatenate([scr[slot + c] for c in range(tiles)], axis=1)

    def per_lane(lse):
        spread = spread_ref[...]
        hi, mid, lo = _split3(lse)
        return _dot(hi, spread) + _dot(mid, spread) + _dot(lo, spread)

    o0, o1, o2 = (natural(r, s * o_tiles, d, o_tiles)
                  for s, (r, d) in enumerate(zip((o0_ref, o1_ref, o2_ref), dils)))
    l0, l1, l2 = (per_lane(natural(r, N_GROUPS * o_tiles + s, d, 1))
                  for s, (r, d) in enumerate(zip((l0_ref, l1_ref, l2_ref), dils)))
    mx = jnp.maximum(jnp.maximum(l0, l1), l2)
    w0, w1, w2 = jnp.exp(l0 - mx), jnp.exp(l1 - mx), jnp.exp(l2 - mx)
    o = (w0 * o0 + w1 * o1 + w2 * o2) / (w0 + w1 + w2)
    out_ref[...] = x_ref[...] + _dot(o.astype(BF16), w_ref[...])


def _a_out(os_, lses, x, w_all, layer, *, tm, dils, side=None):
    m, d = x.shape
    row = lambda i: (i, 0)
    grp = lambda width: [pl.BlockSpec((tm // dil, dil * width), row) for dil in dils]
    lane = jnp.arange(LANES)[:, None]
    col = jnp.arange(A_WIDTH)[None, :]
    spread = (lane == LSE_SEG * (col // A_HEAD_DIM)).astype(BF16)
    return _call(
        functools.partial(_a_out_kernel, dils=dils), (m // tm,),
        grp(A_WIDTH) + grp(LANES) + [pl.BlockSpec((tm, d), row),
                                     _resident((None, A_WIDTH, d), lambda i: (layer, 0, 0)),
                                     pl.BlockSpec((LANES, A_WIDTH), lambda i: (0, 0))],
        [pl.BlockSpec((tm, d), row)], [jax.ShapeDtypeStruct((m, d), F32)],
        [pltpu.VMEM((N_GROUPS * (A_WIDTH // LANES + 1), tm, LANES), F32)],
        [*os_, *lses, x, w_all, spread], "a_out", side)


def _split3(x):
    hi = x.astype(BF16)
    r1 = x - hi.astype(F32)
    mid = r1.astype(BF16)
    lo = (r1 - mid.astype(F32)).astype(BF16)
    return hi, mid, lo


def _decode_attn_body(qt_ref, newt_ref, c_ref, out_ref, ot_ref, lset_ref, first_seq, *, dil, n_rel):
    per_step, _, lg = c_ref.shape
    bd = qt_ref.shape[1]
    reps = lg // LANES
    lane_seq = lax.broadcasted_iota(jnp.int32, (1, bd), 1)
    row_seq = lax.broadcasted_iota(jnp.int32, (bd, LANES), 0)
    head_row = lax.broadcasted_iota(jnp.int32, (A_HEADS, 1), 0)
    pos = lax.broadcasted_iota(jnp.int32, (1, lg), 1)
    back = lg - pos
    valid = ((back & (dil - 1)) == 0) & (back <= n_rel * dil)
    keep = lax.broadcasted_iota(jnp.int32, (1, LANES), 1) < LANES - 1
    tiles = [slice(t * LANES, (t + 1) * LANES) for t in range(reps)]

    def stream(e, rows, use, newcol):
        prev = None
        for t in range(reps):
            x = c_ref[e, rows, tiles[t]]
            use(t, x)
            rot = pltpu.roll(x, LANES - 1, 1)
            if prev is not None:
                out_ref[e, rows, tiles[t - 1]] = jnp.where(keep, prev, rot)
            prev = rot
        out_ref[e, rows, tiles[reps - 1]] = jnp.where(keep, prev, newcol)

    def head_rows(col):
        return jnp.concatenate(
            [jnp.broadcast_to(col[hd:hd + 1, :], (A_HEAD_DIM, 1)) for hd in range(A_HEADS)], axis=0)

    krows = [slice(hd * A_HEAD_DIM, (hd + 1) * A_HEAD_DIM) for hd in range(A_HEADS)]
    vrows = [slice(A_WIDTH + hd * A_HEAD_DIM, A_WIDTH + (hd + 1) * A_HEAD_DIM) for hd in range(A_HEADS)]
    seqs = [first_seq + e for e in range(per_step)]
    qt = qt_ref[...]
    new_hi, new_mid, new_lo = _split3(newt_ref[...])
    qbs, newbs = [], []
    for b in seqs:
        pick = (row_seq == b).astype(BF16)
        qbs.append(_dot(qt, pick))
        newbs.append(_dot(new_hi, pick) + _dot(new_mid, pick) + _dot(new_lo, pick))

    scs, sc_news = [], []
    for e in range(per_step):
        sc_tiles = [jnp.zeros((A_HEADS, LANES), F32) for _ in range(reps)]
        sc_new = jnp.zeros((A_HEADS, LANES), F32)
        for hd in range(A_HEADS):
            qh = qbs[e][krows[hd], :]

            def score(t, kt, hd=hd, qh=qh):
                sc_tiles[t] = jnp.where(head_row == hd,
                                        jnp.sum(kt * qh, axis=0, keepdims=True), sc_tiles[t])

            stream(e, krows[hd], score, newbs[e][krows[hd], :])
            sc_new = jnp.where(head_row == hd,
                               jnp.sum(newbs[e][krows[hd], :] * qh, axis=0, keepdims=True), sc_new)
        sc = sc_tiles[0] if reps == 1 else jnp.concatenate(sc_tiles, axis=1)
        scs.append(jnp.where(valid, sc, NEG))
        sc_news.append(sc_new[:, 0:1])

    ps, p_news, ls, ms = [], [], [], []
    for e in range(per_step):
        m = jnp.maximum(jnp.max(scs[e], axis=1, keepdims=True), sc_news[e])
        p = jnp.where(valid, jnp.exp(scs[e] - m), 0.0)
        p_new = jnp.exp(sc_news[e] - m)
        ps.append(p)
        p_news.append(p_new)
        ms.append(m)
        ls.append(jnp.sum(p, axis=1, keepdims=True) + p_new)

    for e in range(per_step):
        parts = []
        for hd in range(A_HEADS):
            part = [jnp.zeros((A_HEAD_DIM, LANES), F32)]

            def weigh(t, vt, hd=hd, part=part, e=e):
                part[0] = part[0] + vt * ps[e][hd:hd + 1, tiles[t]]

            stream(e, vrows[hd], weigh, newbs[e][vrows[hd], :])
            parts.append(part[0])
        acc = jnp.sum(jnp.concatenate(parts, axis=0), axis=1, keepdims=True)
        o = (acc + head_rows(p_news[e]) * newbs[e][A_WIDTH:, 0:1]) / head_rows(ls[e])
        onehot = lane_seq == seqs[e]
        ot_ref[...] = jnp.where(onehot, o, ot_ref[...])
        lset_ref[...] = jnp.where(onehot, ms[e] + jnp.log(ls[e]), lset_ref[...])


class _DecodeSide:
    def __init__(self, qt, newt, cache, layer, g, carry, first_seq, count, grid, lin):
        nl, bd, rows, lg = cache.shape
        per_step = DECODE_SEQS_PER_STEP[g]
        assert count % per_step == 0 and first_seq % per_step == 0
        n = count // per_step
        host_steps = 1
        for s in grid:
            host_steps *= s
        assert n <= host_steps
        kv_prev, ot_prev, lse_prev = carry
        const = lambda *idx: (0, 0)
        buf_spec = pl.BlockSpec(
            (None, per_step, rows, lg),
            lambda *idx: (layer, first_seq // per_step + jnp.minimum(lin(*idx), n - 1), 0, 0))
        ot_spec = pl.BlockSpec((A_WIDTH, bd), const)
        lse_spec = pl.BlockSpec((A_HEADS, bd), const)
        self.in_specs = [pl.BlockSpec((A_WIDTH, bd), const), pl.BlockSpec((rows, bd), const),
                         buf_spec, ot_spec, lse_spec]
        self.args = [qt, newt, cache, ot_prev, lse_prev]
        self.aliases = {}
        if kv_prev is not None:
            self.in_specs.append(pl.BlockSpec(memory_space=pl.ANY))
            self.args.append(kv_prev)
            self.aliases = {5: 0}
        self.out_specs = [buf_spec, ot_spec, lse_spec]
        self.out_shape = [jax.ShapeDtypeStruct(cache.shape, F32),
                          jax.ShapeDtypeStruct((A_WIDTH, bd), F32),
                          jax.ShapeDtypeStruct((A_HEADS, bd), F32)]

        self.scratch_shapes = []

        def body(ins, outs, scratch):
            qt_ref, newt_ref, c_ref, ot_in, lse_in = ins[:5]
            out_ref, ot_ref, lset_ref = outs
            step = lin(*[pl.program_id(a) for a in range(len(grid))])

            @pl.when(step == 0)
            def _():
                ot_ref[...] = ot_in[...]
                lset_ref[...] = lse_in[...]

            def work():
                _decode_attn_body(qt_ref, newt_ref, c_ref, out_ref, ot_ref, lset_ref,
                                  first_seq + step * per_step,
                                  dil=DILATIONS[g], n_rel=WINDOWS[g] // DILATIONS[g])

            if n < host_steps:
                pl.when(step < n)(work)
            else:
                work()

        self.body = body


def _call(host_kernel, grid, in_specs, out_specs, out_shape, scratch_shapes, args, name, side=None):
    n_in, n_out = len(in_specs), len(out_specs)
    if side is None:
        kernel, aliases, side_args = host_kernel, {}, []
    else:
        n_sin, n_sout = len(side.in_specs), len(side.out_specs)
        n_scr = len(scratch_shapes)

        def kernel(*refs):
            o0 = n_in + n_sin
            s0 = o0 + n_out + n_sout
            if host_kernel is not None:
                host_kernel(*refs[:n_in], *refs[o0:o0 + n_out], *refs[s0:s0 + n_scr])
            side.body(refs[n_in:o0], refs[o0 + n_out:s0], refs[s0 + n_scr:])

        aliases = {n_in + a: n_out + b for a, b in side.aliases.items()}
        in_specs, side_args = in_specs + side.in_specs, side.args
        out_specs, out_shape = out_specs + side.out_specs, out_shape + side.out_shape
        scratch_shapes = scratch_shapes + side.scratch_shapes
    outs = pl.pallas_call(
        kernel, grid=grid, in_specs=in_specs, out_specs=out_specs, out_shape=out_shape,
        scratch_shapes=scratch_shapes, input_output_aliases=aliases,
        compiler_params=_cparams(*(("arbitrary",) * len(grid))), name=name,
    )(*args, *side_args)
    return outs[:n_out], outs[n_out:]


class _Rider:
    def __init__(self, total, per_step, make, carry, name):
        self.total, self.per_step, self.make, self.carry, self.name = total, per_step, make, carry, name
        self.done = 0
        self.pending = False

    def ride(self, grid, lin=lambda i: i):
        steps = 1
        for s in grid:
            steps *= s
        count = min(steps * self.per_step, self.total - self.done)
        self.pending = count > 0
        if not self.pending:
            return None
        side = self.make(self.carry, self.done, count, grid, lin)
        self.done += count
        return side

    def update(self, rode):
        if self.pending:
            self.carry = tuple(rode)
            self.pending = False

    def finish(self):
        if self.done < self.total:
            grid = ((self.total - self.done) // self.per_step,)
            _, rode = _call(None, grid, [], [], [], [], [], self.name, self.ride(grid))
            self.update(rode)
        return self.carry


def _ffn_kernel(x_ref, g_ref, wg_ref, wu_ref, wd_ref, out_ref, *, chunk):
    x = x_ref[...]
    h = _rms(x, g_ref[...]).astype(BF16)
    out_ref[...] = x
    for c in range(wg_ref.shape[1] // chunk):
        cols = slice(c * chunk, (c + 1) * chunk)
        gate = _dot(h, wg_ref[:, cols])
        up = _dot(h, wu_ref[:, cols])
        out_ref[...] += _dot((_silu(gate) * up).astype(BF16), wd_ref[cols, :])


def _ffn(x, g, wg_all, wu_all, wd_all, layer, *, tm, side=None):
    m, d = x.shape
    f = wg_all.shape[2]
    row = lambda i: (i, 0)
    sel = lambda i: (layer, 0, 0)
    (out,), rode = _call(
        functools.partial(_ffn_kernel, chunk=MXU_DIM), (m // tm,),
        [pl.BlockSpec((tm, d), row), pl.BlockSpec((1, d), lambda i: (0, 0)),
         _resident((None, d, f), sel), _resident((None, d, f), sel), _resident((None, f, d), sel)],
        [pl.BlockSpec((tm, d), row)], [jax.ShapeDtypeStruct((m, d), F32)], [],
        [x, g, wg_all, wu_all, wd_all], "ffn", side)
    return out, rode


def _r_proj_kernel(x_ref, gmix_ref, w_ref, cos_ref, sin_ref, out_ref, *qkt_ref, qk_dim):
    h = _rms(x_ref[...], gmix_ref[...]).astype(BF16)
    half = qk_dim // 2
    nqk = 2 * R_HEADS * qk_dim
    cos, sin = cos_ref[...], sin_ref[...]
    p = _dot(h, w_ref[:, 0:nqk])
    for hd in range(2 * R_HEADS):
        scale = 1.0 if hd < R_HEADS else qk_dim ** -0.5
        c0 = hd * qk_dim
        x1, x2 = p[:, c0:c0 + half], p[:, c0 + half:c0 + qk_dim]
        o1 = ((x1 * cos - x2 * sin) * scale).astype(BF16)
        o2 = ((x2 * cos + x1 * sin) * scale).astype(BF16)
        out_ref[:, c0:c0 + half] = o1
        out_ref[:, c0 + half:c0 + qk_dim] = o2
        if qkt_ref:
            qkt_ref[0][c0:c0 + half, :] = o1.astype(F32).T.astype(BF16)
            qkt_ref[0][c0 + half:c0 + qk_dim, :] = o2.astype(F32).T.astype(BF16)
    for c0 in range(nqk, w_ref.shape[1], nqk):
        out_ref[:, c0:c0 + nqk] = _dot(h, w_ref[:, c0:c0 + nqk]).astype(BF16)


def _r_proj(x, gmix, w_all, layer, cos, sin, *, tm, transposed_qk=False, side=None):
    m, d = x.shape
    n = w_all.shape[2]
    qk_dim = n // (6 * R_HEADS)
    nqk = 2 * R_HEADS * qk_dim
    ntab = cos.shape[0] // tm
    tab = pl.BlockSpec((tm, qk_dim // 2), lambda i: (i % ntab, 0))
    out_specs = [pl.BlockSpec((tm, n), lambda i: (i, 0))]
    out_shape = [jax.ShapeDtypeStruct((m, n), BF16)]
    if transposed_qk:
        out_specs.append(pl.BlockSpec((nqk, tm), lambda i: (0, i)))
        out_shape.append(jax.ShapeDtypeStruct((nqk, m), BF16))
    return _call(
        functools.partial(_r_proj_kernel, qk_dim=qk_dim), (m // tm,),
        [pl.BlockSpec((tm, d), lambda i: (i, 0)),
         pl.BlockSpec((1, d), lambda i: (0, 0)),
         _resident((None, d, n), lambda i: (layer, 0, 0)), tab, tab],
        out_specs, out_shape, [], [x, gmix, w_all, cos, sin], "r_proj", side)


def _retention_kernel(p_ref, x_ref, dmask_ref, qdec_ref, kdec_ref, cdec_ref, gout_ref, wo_ref,
                      out_ref, state_ref, y_scr, *, qk, dv):
    c = pl.program_id(0)

    @pl.when(c == 0)
    def _():
        state_ref[...] = jnp.zeros_like(state_ref)

    v0 = 2 * R_HEADS * qk
    g0 = v0 + R_HEADS * dv
    for b in range(p_ref.shape[0]):
        for hd in range(R_HEADS):
            q = p_ref[b, :, hd * qk:(hd + 1) * qk]
            k = p_ref[b, :, (R_HEADS + hd) * qk:(R_HEADS + hd + 1) * qk]
            v = p_ref[b, :, v0 + hd * dv:v0 + (hd + 1) * dv]
            gate = p_ref[b, :, g0 + hd * dv:g0 + (hd + 1) * dv].astype(F32)
            s_prev = state_ref[b, hd]
            inter = _dot(q, s_prev.astype(BF16)) * qdec_ref[hd]
            scores = _dot_nt(q, k) * dmask_ref[hd]
            o = inter + _dot(scores.astype(BF16), v)
            kd = (k.astype(F32) * kdec_ref[hd]).astype(BF16)
            state_ref[b, hd] = s_prev * cdec_ref[hd][:, 0:1] + _dot_tn(kd, v)
            y = _rms(o, gout_ref[hd]) * _silu(gate)
            y_scr[b, :, hd * dv:(hd + 1) * dv] = y.astype(BF16)
        out_ref[b] = x_ref[b] + _dot(y_scr[b], wo_ref[...])


def _retention_tables(chunk):
    lg = jnp.log1p(-jnp.exp2(-5.0 - jnp.arange(R_HEADS, dtype=F32)))
    idx = jnp.arange(chunk, dtype=F32)
    rel = idx[:, None] - idx[None, :]
    dmask = jnp.where(rel >= 0, jnp.exp(jnp.maximum(rel, 0.0) * lg[:, None, None]), 0.0)
    qdec = jnp.exp((idx + 1.0)[None, :] * lg[:, None])[:, :, None]
    kdec = jnp.exp((chunk - 1.0 - idx)[None, :] * lg[:, None])[:, :, None]
    cdec = jnp.broadcast_to(jnp.exp(chunk * lg)[:, None, None], (R_HEADS, 1, LANES))
    return dmask, qdec, kdec, cdec


def _retention(p, x, gout, wo_all, layer, batch, seq, *, chunk, side=None):
    m, n = p.shape
    d = x.shape[1]
    qk = n // (6 * R_HEADS)
    dv = 2 * qk
    tables = _retention_tables(chunk)
    full = lambda a: pl.BlockSpec(a.shape, lambda c: (0,) * a.ndim)
    rows = lambda width: pl.BlockSpec((batch, chunk, width), lambda c: (0, c, 0))
    (out, state), rode = _call(
        functools.partial(_retention_kernel, qk=qk, dv=dv), (seq // chunk,),
        [rows(n), rows(d)] + [full(t) for t in tables] + [
            full(gout), _resident((None, R_HEADS * dv, d), lambda c: (layer, 0, 0))],
        [rows(d), pl.BlockSpec((batch, R_HEADS, qk, dv), lambda c: (0, 0, 0, 0))],
        [jax.ShapeDtypeStruct((batch, seq, d), F32),
         jax.ShapeDtypeStruct((batch, R_HEADS, qk, dv), F32)],
        [pltpu.VMEM((batch, chunk, R_HEADS * dv), BF16)],
        [p.reshape(batch, seq, n), x.reshape(batch, seq, d), *tables, gout, wo_all],
        "retention", side)
    return out.reshape(m, d), state, rode


def _retention_step_body(p_ref, qkt_ref, s_ref, gout_ref, y_ref, snew_ref, first_seq, *, qk, dv):
    per_step = s_ref.shape[0]
    bd = qkt_ref.shape[1]
    row_seq = lax.broadcasted_iota(jnp.int32, (bd, LANES), 0)
    qkt = qkt_ref[...]
    v0 = 2 * R_HEADS * qk
    g0 = v0 + R_HEADS * dv
    for e in range(per_step):
        pick = (row_seq == first_seq + e).astype(BF16)
        qkb = _dot(qkt, pick)
        for hd in range(R_HEADS):
            gamma = 1.0 - 2.0 ** (-5 - hd)
            qb = qkb[hd * qk:(hd + 1) * qk, :]
            kb = qkb[(R_HEADS + hd) * qk:(R_HEADS + hd + 1) * qk, :]
            qk_dot = jnp.sum(qb * kb, axis=0, keepdims=True)
            o_tiles = []
            for t in range(dv // LANES):
                lanes = slice(t * LANES, (t + 1) * LANES)
                v = p_ref[e, :, v0 + hd * dv + t * LANES:v0 + hd * dv + (t + 1) * LANES].astype(F32)
                s_prev = s_ref[e, hd, :, lanes]
                inter = jnp.sum(s_prev * qb, axis=0, keepdims=True) * gamma
                o_tiles.append(inter + qk_dot * v)
                snew_ref[e, hd, :, lanes] = s_prev * gamma + kb * v
            o = jnp.concatenate(o_tiles, axis=1)
            gate = p_ref[e, :, g0 + hd * dv:g0 + (hd + 1) * dv].astype(F32)
            y = _rms(o, gout_ref[hd]) * _silu(gate)
            y_ref[e, :, hd * dv:(hd + 1) * dv] = y.astype(BF16)


class _RetentionSide:
    def __init__(self, p, qkt, state, layer, gout, carry, first_seq, count, grid, lin):
        bd = p.shape[0]
        nl, _, nh, qk, dv = state.shape
        per_step = RETENTION_SEQS_PER_STEP
        assert count % per_step == 0 and first_seq % per_step == 0
        n = count // per_step
        host_steps = 1
        for s in grid:
            host_steps *= s
        assert n <= host_steps
        blk = lambda *idx: first_seq // per_step + jnp.minimum(lin(*idx), n - 1)
        state_spec = pl.BlockSpec((None, per_step, nh, qk, dv),
                                  lambda *idx: (layer, blk(*idx), 0, 0, 0))
        y_spec = pl.BlockSpec((per_step, 1, nh * dv), lambda *idx: (blk(*idx), 0, 0))
        self.in_specs = [pl.BlockSpec((per_step, 1, p.shape[2]), lambda *idx: (blk(*idx), 0, 0)),
                         pl.BlockSpec(qkt.shape, lambda *idx: (0, 0)),
                         state_spec,
                         pl.BlockSpec((nh, 1, dv), lambda *idx: (0, 0, 0))]
        self.args = [p, qkt, state, gout]
        self.aliases = {}
        for out_idx, prev in enumerate(carry):
            if prev is not None:
                self.aliases[len(self.args)] = out_idx
                self.in_specs.append(pl.BlockSpec(memory_space=pl.ANY))
                self.args.append(prev)
        self.out_specs = [y_spec, state_spec]
        self.out_shape = [jax.ShapeDtypeStruct((bd, 1, nh * dv), BF16),
                          jax.ShapeDtypeStruct(state.shape, F32)]

        self.scratch_shapes = []

        def body(ins, outs, scratch):
            step = lin(*[pl.program_id(a) for a in range(len(grid))])

            def work():
                _retention_step_body(*ins[:4], *outs, first_seq + step * per_step, qk=qk, dv=dv)

            if n < host_steps:
                pl.when(step < n)(work)
            else:
                work()

        self.body = body


def _proj_res_kernel(y_ref, x_ref, w_ref, out_ref):
    out_ref[...] = x_ref[...] + _dot(y_ref[...], w_ref[...])


def _proj_res(y, x, w_all, layer, *, tm):
    m, d = x.shape
    kdim = y.shape[1]
    row = lambda i: (i, 0)
    return pl.pallas_call(
        _proj_res_kernel,
        grid=(m // tm,),
        in_specs=[pl.BlockSpec((tm, kdim), row), pl.BlockSpec((tm, d), row),
                  _resident((None, kdim, d), lambda i: (layer, 0, 0))],
        out_specs=pl.BlockSpec((tm, d), row),
        out_shape=jax.ShapeDtypeStruct((m, d), F32),
        compiler_params=_cparams("arbitrary"),
        name="proj_res",
    )(y, x, w_all)


def _rope_tables(pos, half, width, sign_fold):
    lane = jnp.arange(width)
    inv = ROPE_THETA ** (-(lane % half).astype(F32) / half)
    ang = pos.astype(F32)[:, None] * inv[None, :]
    cos, sin = jnp.cos(ang), jnp.sin(ang)
    if sign_fold:
        sin = jnp.where((lane % (2 * half)) < half, -sin, sin)
    return cos, sin


def kernel(x_prompt, x_sample, cache_kv_g0, cache_kv_g1, cache_kv_g2, state_retention,
           mix_norm, ffn_norm, a_w_qkv, a_q_norm, a_k_norm, a_w_o,
           r_w_in, r_out_norm, r_w_o, f_w_gate, f_w_up, f_w_down):
    batch, seq, d = x_prompt.shape
    bd, dec_seq, _ = x_sample.shape
    assert dec_seq == 1
    depth = mix_norm.shape[0]
    caches = (cache_kv_g0, cache_kv_g1, cache_kv_g2)
    keeps = tuple(min(w, seq) for w in WINDOWS)
    tm = ROW_TILE
    qk_dim = r_w_in.shape[2] // (6 * R_HEADS)

    xp = x_prompt.reshape(batch * seq, d)
    xs = x_sample.reshape(bd, d)

    caches_t = [jnp.transpose(c, (0, 1, 3, 4, 5, 2)).reshape(c.shape[0], bd, 2 * A_WIDTH, c.shape[2])
                for c in caches]

    a_w_qkv, a_w_o, r_w_in, r_w_o, f_w_gate, f_w_up, f_w_down = (
        w.astype(BF16) for w in (a_w_qkv, a_w_o, r_w_in, r_w_o, f_w_gate, f_w_up, f_w_down))

    pos_p = jnp.arange(seq)
    pos_s = jnp.full((bd,), PAST_LEN)
    a_tabs_p = _rope_tables(pos_p, A_HEAD_DIM // 2, LANES, True)
    a_tabs_s = _rope_tables(pos_s, A_HEAD_DIM // 2, LANES, True)
    r_tabs_p = _rope_tables(pos_p, qk_dim // 2, qk_dim // 2, False)
    r_tabs_s = _rope_tables(pos_s, qk_dim // 2, qk_dim // 2, False)
    lane = jnp.arange(MXU_DIM)
    ones_bd = (lane[:, None] // A_HEAD_DIM == lane[None, :] // A_HEAD_DIM).astype(BF16)
    no_dil = (1,) * N_GROUPS

    kv_p = [[] for _ in range(N_GROUPS)]
    kv_s = [None] * N_GROUPS
    ret_p, ret_s = [], None
    p_ahead = None
    for i in range(depth):
        j = i // 2
        gmix = mix_norm[i][None, :]
        if i % 2 == 0:
            gq = jnp.tile(a_q_norm[j], (1, A_HEADS))[:, None, :]
            gk = jnp.tile(a_k_norm[j], (1, A_HEADS))[:, None, :]
            qts, newts, _ = _a_proj(xs, gmix, a_w_qkv, j, gq, gk, *a_tabs_s, ones_bd, tm=bd, seq=bd,
                                    keeps=(bd,) * N_GROUPS, dils=no_dil, transposed_q=True)
            riders = [
                _Rider(bd, DECODE_SEQS_PER_STEP[g],
                       functools.partial(_DecodeSide, qts[g], newts[g][0], caches_t[g], j, g),
                       (kv_s[g], jnp.zeros((A_WIDTH, bd), F32), jnp.zeros((A_HEADS, bd), F32)),
                       f"decode_attn_g{g}")
                for g in range(N_GROUPS)]
            qkv, kvt, rode = _a_proj(xp, gmix, a_w_qkv, j, gq, gk, *a_tabs_p, ones_bd, tm=tm,
                                     seq=seq, keeps=keeps, dils=DILATIONS, transposed_q=False,
                                     side=riders[0].ride((batch * seq // tm,)))
            riders[0].update(rode)
            os_, lses = [], []
            for g in range(N_GROUPS):
                (o, lse), rode = _band_attn(*qkv[3 * g:3 * g + 3], g, batch, seq,
                                            riders[2].ride(*_band_attn_grid(g, batch, seq)))
                riders[2].update(rode)
                os_.append(o)
                lses.append(lse)
                kv_p[g].append(kvt[g])
            (xp,), rode = _a_out(os_, lses, xp, a_w_o, j, tm=A_OUT_TILE, dils=DILATIONS,
                                 side=riders[2].ride((batch * seq // A_OUT_TILE,)))
            riders[2].update(rode)
            ffn_rider = riders[1]

            def sample_mixer(xs):
                os_, lses = [], []
                for g in range(N_GROUPS):
                    kv_s[g], ot, lset = riders[g].finish()
                    os_.append(ot.T)
                    lses.append(jnp.repeat(lset.T, LSE_SEG, axis=1))
                (xs,), _ = _a_out(os_, lses, xs, a_w_o, j, tm=bd, dils=no_dil)
                return xs
        else:
            gout = r_out_norm[j][:, None, :]
            (ps, qkt), _ = _r_proj(xs, gmix, r_w_in, j, *r_tabs_s, tm=bd, transposed_qk=True)
            ffn_rider = _Rider(
                bd, RETENTION_SEQS_PER_STEP,
                functools.partial(_RetentionSide, ps[:, None, :], qkt, state_retention, j, gout),
                (None, ret_s), "retention_step")
            p = p_ahead
            if p is None:
                (p,), _ = _r_proj(xp, gmix, r_w_in, j, *r_tabs_p, tm=tm)
            xp, s_fin, rode = _retention(p, xp, gout, r_w_o, j, batch, seq, chunk=R_CHUNK,
                                         side=ffn_rider.ride((seq // R_CHUNK,)))
            ffn_rider.update(rode)
            ret_p.append(s_fin)

            def sample_mixer(xs):
                nonlocal ret_s
                ys, ret_s = ffn_rider.finish()
                return _proj_res(ys.reshape(bd, -1), xs, r_w_o, j, tm=bd)
        gffn = ffn_norm[i][None, :]
        xp, rode = _ffn(xp, gffn, f_w_gate, f_w_up, f_w_down, i, tm=tm,
                        side=ffn_rider.ride((batch * seq // tm,)))
        ffn_rider.update(rode)
        p_ahead = None
        if i % 2 == 0 and i + 1 < depth:
            (p_ahead,), rode = _r_proj(xp, mix_norm[i + 1][None, :], r_w_in, (i + 1) // 2, *r_tabs_p,
                                       tm=tm, side=ffn_rider.ride((batch * seq // tm,)))
            ffn_rider.update(rode)
        xs = sample_mixer(xs)
        xs, _ = _ffn(xs, gffn, f_w_gate, f_w_up, f_w_down, i, tm=bd)

    def rows_major(t):
        lead, slots = t.shape[:-2], t.shape[-1]
        t = t.reshape(lead + (2, A_HEADS, A_HEAD_DIM, slots))
        return jnp.moveaxis(t, -1, len(lead))

    return (xp.reshape(batch, seq, d), xs.reshape(bd, 1, d),
            rows_major(jnp.stack(kv_p[0])), rows_major(kv_s[0]),
            rows_major(jnp.stack(kv_p[1])), rows_major(kv_s[1]),
            rows_major(jnp.stack(kv_p[2])), rows_major(kv_s[2]),
            jnp.stack(ret_p), ret_s)
```

```python
import functools

import jax
import jax.numpy as jnp
import numpy as np
from jax import lax
from jax.experimental import pallas as pl
from jax.experimental.pallas import tpu as pltpu

F32 = jnp.float32
BF16 = jnp.bfloat16

EPS = 1e-6
ROPE_THETA = 10000.0
WINDOWS = (128, 512, 2048)
DILATIONS = (1, 4, 16)
N_GROUPS = 3
A_HEADS = 8
A_HEAD_DIM = 64
A_WIDTH = A_HEADS * A_HEAD_DIM
A_BLOCK = 128
R_HEADS = 4
R_CHUNK = 256
PAST_LEN = 2048
NEG = -1e30

VMEM_LIMIT_BYTES = 56 * 1024 * 1024
LANES = 128
MXU_DIM = 256
LSE_SEG = LANES // A_HEADS
ROW_TILE = 512
FFN_CHUNK = MXU_DIM
A_OUT_TILE = 256
DECODE_SEQS_PER_STEP = (4, 2, 1)
RETENTION_SEQS_PER_STEP = 2


def _cparams(*sem):
    return pltpu.CompilerParams(dimension_semantics=sem, vmem_limit_bytes=VMEM_LIMIT_BYTES)


def _resident(shape, index_map):
    return pl.BlockSpec(shape, index_map, pipeline_mode=pl.Buffered(1))


def _rms(x, g):
    ms = jnp.mean(x * x, axis=-1, keepdims=True)
    return x * lax.rsqrt(ms + EPS) * g


def _silu(x):
    return x * (1.0 / (1.0 + jnp.exp(-x)))


def _dot(a, b):
    return jnp.dot(a, b, preferred_element_type=F32)


def _dot_nt(a, b):
    return lax.dot_general(a, b, (((1,), (1,)), ((), ())), preferred_element_type=F32)


def _dot_tn(a, b):
    return lax.dot_general(a, b, (((0,), (0,)), ((), ())), preferred_element_type=F32)


def _a_proj_kernel(x_ref, gmix_ref, w_ref, gq_ref, gk_ref, cos_ref, sin_ref, ones_ref, *refs,
                   dils, tail_blocks, tiles_per_seq, transposed_q):
    n_qkv = N_GROUPS if transposed_q else 3 * N_GROUPS
    qkv_refs, kvt_refs, deint = refs[:n_qkv], refs[n_qkv:n_qkv + N_GROUPS], refs[n_qkv + N_GROUPS]
    tm = x_ref.shape[0]
    tile_in_seq = pl.program_id(0) % tiles_per_seq
    h = _rms(x_ref[...], gmix_ref[...]).astype(BF16)
    cos = jnp.concatenate([cos_ref[...]] * (A_WIDTH // LANES), axis=1)
    sin = jnp.concatenate([sin_ref[...]] * (A_WIDTH // LANES), axis=1)
    lane = lax.broadcasted_iota(jnp.int32, (1, A_WIDTH), 1)
    first_half = (lane % A_HEAD_DIM) < (A_HEAD_DIM // 2)
    ones = ones_ref[...]

    def normed_rope(p, gain):
        y = (p * p).astype(BF16)
        ss = jnp.concatenate([_dot(y[:, :MXU_DIM], ones), _dot(y[:, MXU_DIM:], ones)], axis=1)
        pn = p * lax.rsqrt(ss * (1.0 / A_HEAD_DIM) + EPS) * gain
        rot = jnp.where(first_half,
                        pltpu.roll(pn, A_WIDTH - A_HEAD_DIM // 2, 1),
                        pltpu.roll(pn, A_HEAD_DIM // 2, 1))
        return pn * cos + rot * sin

    def store_dilated(out_ref, slot, val, dil):
        if dil == 1:
            out_ref[...] = val.astype(BF16)
            return
        tiles = A_WIDTH // LANES
        for c in range(tiles):
            deint[slot * tiles + c] = val[:, c * LANES:(c + 1) * LANES]
        for r in range(dil):
            for c in range(tiles):
                out_ref[:, r * A_WIDTH + c * LANES:r * A_WIDTH + (c + 1) * LANES] = (
                    deint[slot * tiles + c, pl.ds(r, tm // dil, stride=dil), :].astype(BF16))

    for g in range(N_GROUPS):
        c0 = 3 * g * A_WIDTH
        q = normed_rope(_dot(h, w_ref[:, c0:c0 + A_WIDTH]), gq_ref[g]) * (A_HEAD_DIM ** -0.5)
        k = normed_rope(_dot(h, w_ref[:, c0 + A_WIDTH:c0 + 2 * A_WIDTH]), gk_ref[g])
        v = _dot(h, w_ref[:, c0 + 2 * A_WIDTH:c0 + 3 * A_WIDTH])
        if transposed_q:
            qkv_refs[g][...] = q.T.astype(BF16)
        else:
            store_dilated(qkv_refs[3 * g], 0, q, dils[g])
            store_dilated(qkv_refs[3 * g + 1], 1, k, dils[g])
            store_dilated(qkv_refs[3 * g + 2], 2, v, dils[g])

        kvt_ref = kvt_refs[g]
        tw = kvt_ref.shape[1]

        @pl.when(tile_in_seq >= tiles_per_seq - tail_blocks[g])
        def _(k=k, v=v, kvt_ref=kvt_ref, tw=tw):
            kvt_ref[0:A_WIDTH, :] = k[tm - tw:, :].T
            kvt_ref[A_WIDTH:, :] = v[tm - tw:, :].T


def _a_proj(x, gmix, w_all, layer, gq, gk, cos, sin, ones, *, tm, seq, keeps, dils, transposed_q,
            side=None):
    m, d = x.shape
    n_seq = m // seq
    tps = seq // tm
    ntab = cos.shape[0] // tm
    tws = [min(tm, kp) for kp in keeps]
    tail_blocks = tuple(kp // tw for kp, tw in zip(keeps, tws))

    def kvt_map(nb):
        return lambda i: (i // tps, 0, jnp.maximum(i % tps - (tps - nb), 0))

    if transposed_q:
        qkv_specs = [pl.BlockSpec((A_WIDTH, tm), lambda i: (0, i))] * N_GROUPS
        qkv_shapes = [jax.ShapeDtypeStruct((A_WIDTH, m), BF16)] * N_GROUPS
    else:
        qkv_specs, qkv_shapes = [], []
        for dil in dils:
            qkv_specs += [pl.BlockSpec((tm // dil, dil * A_WIDTH), lambda i: (i, 0))] * 3
            qkv_shapes += [jax.ShapeDtypeStruct((m // dil, dil * A_WIDTH), BF16)] * 3
    kvt_specs = [pl.BlockSpec((None, 2 * A_WIDTH, tw), kvt_map(nb)) for tw, nb in zip(tws, tail_blocks)]
    kvt_shapes = [jax.ShapeDtypeStruct((n_seq, 2 * A_WIDTH, kp), F32) for kp in keeps]
    gain_spec = pl.BlockSpec((N_GROUPS, 1, A_WIDTH), lambda i: (0, 0, 0))
    tab_spec = pl.BlockSpec((tm, LANES), lambda i: (i % ntab, 0))
    outs, rode = _call(
        functools.partial(_a_proj_kernel, dils=dils, tail_blocks=tail_blocks, tiles_per_seq=tps,
                          transposed_q=transposed_q),
        (m // tm,),
        [pl.BlockSpec((tm, d), lambda i: (i, 0)),
         pl.BlockSpec((1, d), lambda i: (0, 0)),
         _resident((None, d, 3 * N_GROUPS * A_WIDTH), lambda i: (layer, 0, 0)),
         gain_spec, gain_spec, tab_spec, tab_spec,
         pl.BlockSpec((MXU_DIM, MXU_DIM), lambda i: (0, 0))],
        qkv_specs + kvt_specs, qkv_shapes + kvt_shapes,
        [pltpu.VMEM((3 * A_WIDTH // LANES, tm, LANES), F32)],
        [x, gmix, w_all, gq, gk, cos, sin, ones], "a_proj", side)
    return outs[:-N_GROUPS], outs[-N_GROUPS:], rode


def _band_attn_kernel(q_ref, kp_ref, kc_ref, vp_ref, vc_ref, o_ref, lse_ref, kb, vb, *, nsub):
    i = pl.program_id(2)
    kb[0:A_BLOCK, :] = kp_ref[...]
    kb[A_BLOCK:, :] = kc_ref[...]
    vb[0:A_BLOCK, :] = vp_ref[...]
    vb[A_BLOCK:, :] = vc_ref[...]
    heads_per_half = MXU_DIM // A_HEAD_DIM
    stacked = (heads_per_half * A_BLOCK, 2 * A_BLOCK)
    qi = lax.broadcasted_iota(jnp.int32, stacked, 0) & (A_BLOCK - 1)
    kj = lax.broadcasted_iota(jnp.int32, stacked, 1)
    band = (kj >= qi) & (kj <= qi + A_BLOCK)
    band_first = band & ((kj >= A_BLOCK) | (i > 0))
    head_of_lane = lax.broadcasted_iota(jnp.int32, (1, MXU_DIM), 1) // A_HEAD_DIM

    seg_of_lane = lax.broadcasted_iota(jnp.int32, (1, LANES), 1) // LSE_SEG

    for s in range(nsub):
        valid = band_first if s == 0 else band
        rows = slice(s * A_BLOCK, (s + 1) * A_BLOCK)
        keys = slice(s * A_BLOCK, (s + 2) * A_BLOCK)
        lse_all = jnp.zeros((A_BLOCK, LANES), F32)
        for half in range(A_WIDTH // MXU_DIM):
            lanes = slice(half * MXU_DIM, (half + 1) * MXU_DIM)
            qh = q_ref[rows, lanes]
            qs = jnp.concatenate(
                [jnp.where(head_of_lane == hd, qh, jnp.zeros_like(qh))
                 for hd in range(heads_per_half)], axis=0)
            sc = _dot_nt(qs, kb[keys, lanes])
            sc = jnp.where(valid, sc, NEG)
            m = jnp.max(sc, axis=1, keepdims=True)
            p = jnp.exp(sc - m)
            l = jnp.sum(p, axis=1, keepdims=True)
            r = _dot(p.astype(BF16), vb[keys, lanes]) * (1.0 / l)
            lse = m + jnp.log(l)
            o_half = jnp.zeros((A_BLOCK, MXU_DIM), F32)
            for hd in range(heads_per_half):
                hrows = slice(hd * A_BLOCK, (hd + 1) * A_BLOCK)
                o_half = jnp.where(head_of_lane == hd, r[hrows, :], o_half)
                lse_all = jnp.where(seg_of_lane == half * heads_per_half + hd, lse[hrows, :], lse_all)
            o_ref[rows, lanes] = o_half.astype(BF16)
        lse_ref[rows, :] = lse_all


def _band_attn_grid(g, batch, seq):
    dil = DILATIONS[g]
    sub = seq // dil
    tq = min(4 * A_BLOCK, sub)
    nb = sub // tq
    return (batch, dil, nb), (lambda b, r, i: (b * dil + r) * nb + i)


def _band_attn(q, k, v, g, batch, seq, side=None):
    grid, _ = _band_attn_grid(g, batch, seq)
    nb = grid[2]
    tq = seq // DILATIONS[g] // nb
    nsub = tq // A_BLOCK
    cur = pl.BlockSpec((tq, A_WIDTH), lambda b, r, i: (b * nb + i, r))
    prev = pl.BlockSpec((A_BLOCK, A_WIDTH),
                        lambda b, r, i: (b * nb * nsub + jnp.maximum(i * nsub - 1, 0), r))
    lse_spec = pl.BlockSpec((tq, LANES), lambda b, r, i: (b * nb + i, r))
    return _call(
        functools.partial(_band_attn_kernel, nsub=nsub), grid,
        [cur, prev, cur, prev, cur], [cur, lse_spec],
        [jax.ShapeDtypeStruct(q.shape, BF16),
         jax.ShapeDtypeStruct((q.shape[0], DILATIONS[g] * LANES), F32)],
        [pltpu.VMEM((tq + A_BLOCK, A_WIDTH), BF16), pltpu.VMEM((tq + A_BLOCK, A_WIDTH), BF16)],
        [q, k, k, v, v], f"band_attn_g{g}", side)


def _a_out_kernel(o0_ref, o1_ref, o2_ref, l0_ref, l1_ref, l2_ref, x_ref, w_ref, spread_ref,
                  out_ref, scr, *, dils):
    tm = x_ref.shape[0]
    o_tiles = A_WIDTH // LANES

    def natural(ref, slot, dil, tiles):
        if dil == 1:
            return ref[...].astype(F32)
        for r in range(dil):
            for c in range(tiles):
                scr[slot + c, pl.ds(r, tm // dil, stride=dil), :] = (
                    ref[:, (r * tiles + c) * LANES:(r * tiles + c + 1) * LANES].astype(F32))
        if tiles == 1:
            return scr[slot]
        return jnp.concatenate([scr[slot + c] for c in range(tiles)], axis=1)

    def per_lane(lse):
        spread = spread_ref[...]
        hi, mid, lo = _split3(lse)
        return _dot(hi, spread) + _dot(mid, spread) + _dot(lo, spread)

    o0, o1, o2 = (natural(r, s * o_tiles, d, o_tiles)
                  for s, (r, d) in enumerate(zip((o0_ref, o1_ref, o2_ref), dils)))
    l0, l1, l2 = (per_lane(natural(r, N_GROUPS * o_tiles + s, d, 1))
                  for s, (r, d) in enumerate(zip((l0_ref, l1_ref, l2_ref), dils)))
    mx = jnp.maximum(jnp.maximum(l0, l1), l2)
    w0, w1, w2 = jnp.exp(l0 - mx), jnp.exp(l1 - mx), jnp.exp(l2 - mx)
    o = (w0 * o0 + w1 * o1 + w2 * o2) / (w0 + w1 + w2)
    out_ref[...] = x_ref[...] + _dot(o.astype(BF16), w_ref[...])


def _a_out(os_, lses, x, w_all, layer, *, tm, dils, side=None):
    m, d = x.shape
    row = lambda i: (i, 0)
    grp = lambda width: [pl.BlockSpec((tm // dil, dil * width), row) for dil in dils]
    lane = jnp.arange(LANES)[:, None]
    col = jnp.arange(A_WIDTH)[None, :]
    spread = (lane == LSE_SEG * (col // A_HEAD_DIM)).astype(BF16)
    return _call(
        functools.partial(_a_out_kernel, dils=dils), (m // tm,),
        grp(A_WIDTH) + grp(LANES) + [pl.BlockSpec((tm, d), row),
                                     _resident((None, A_WIDTH, d), lambda i: (layer, 0, 0)),
                                     pl.BlockSpec((LANES, A_WIDTH), lambda i: (0, 0))],
        [pl.BlockSpec((tm, d), row)], [jax.ShapeDtypeStruct((m, d), F32)],
        [pltpu.VMEM((N_GROUPS * (A_WIDTH // LANES + 1), tm, LANES), F32)],
        [*os_, *lses, x, w_all, spread], "a_out", side)


def _split3(x):
    hi = x.astype(BF16)
    r1 = x - hi.astype(F32)
    mid = r1.astype(BF16)
    lo = (r1 - mid.astype(F32)).astype(BF16)
    return hi, mid, lo


def _decode_attn_body(qt_ref, newt_ref, c_ref, out_ref, ot_ref, lset_ref, first_seq, *, dil, n_rel):
    per_step, _, lg = c_ref.shape
    bd = qt_ref.shape[1]
    reps = lg // LANES
    lane_seq = lax.broadcasted_iota(jnp.int32, (1, bd), 1)
    row_seq = lax.broadcasted_iota(jnp.int32, (bd, LANES), 0)
    head_row = lax.broadcasted_iota(jnp.int32, (A_HEADS, 1), 0)
    pos = lax.broadcasted_iota(jnp.int32, (1, lg), 1)
    back = lg - pos
    valid = ((back & (dil - 1)) == 0) & (back <= n_rel * dil)
    keep = lax.broadcasted_iota(jnp.int32, (1, LANES), 1) < LANES - 1
    tiles = [slice(t * LANES, (t + 1) * LANES) for t in range(reps)]

    def stream(e, rows, use, newcol):
        prev = None
        for t in range(reps):
            x = c_ref[e, rows, tiles[t]]
            use(t, x)
            rot = pltpu.roll(x, LANES - 1, 1)
            if prev is not None:
                out_ref[e, rows, tiles[t - 1]] = jnp.where(keep, prev, rot)
            prev = rot
        out_ref[e, rows, tiles[reps - 1]] = jnp.where(keep, prev, newcol)

    def head_rows(col):
        return jnp.concatenate(
            [jnp.broadcast_to(col[hd:hd + 1, :], (A_HEAD_DIM, 1)) for hd in range(A_HEADS)], axis=0)

    krows = [slice(hd * A_HEAD_DIM, (hd + 1) * A_HEAD_DIM) for hd in range(A_HEADS)]
    vrows = [slice(A_WIDTH + hd * A_HEAD_DIM, A_WIDTH + (hd + 1) * A_HEAD_DIM) for hd in range(A_HEADS)]
    seqs = [first_seq + e for e in range(per_step)]
    qt = qt_ref[...]
    new_hi, new_mid, new_lo = _split3(newt_ref[...])
    qbs, newbs = [], []
    for b in seqs:
        pick = (row_seq == b).astype(BF16)
        qbs.append(_dot(qt, pick))
        newbs.append(_dot(new_hi, pick) + _dot(new_mid, pick) + _dot(new_lo, pick))

    scs, sc_news = [], []
    for e in range(per_step):
        sc_tiles = [jnp.zeros((A_HEADS, LANES), F32) for _ in range(reps)]
        sc_new = jnp.zeros((A_HEADS, LANES), F32)
        for hd in range(A_HEADS):
            qh = qbs[e][krows[hd], :]

            def score(t, kt, hd=hd, qh=qh):
                sc_tiles[t] = jnp.where(head_row == hd,
                                        jnp.sum(kt * qh, axis=0, keepdims=True), sc_tiles[t])

            stream(e, krows[hd], score, newbs[e][krows[hd], :])
            sc_new = jnp.where(head_row == hd,
                               jnp.sum(newbs[e][krows[hd], :] * qh, axis=0, keepdims=True), sc_new)
        sc = sc_tiles[0] if reps == 1 else jnp.concatenate(sc_tiles, axis=1)
        scs.append(jnp.where(valid, sc, NEG))
        sc_news.append(sc_new[:, 0:1])

    ps, p_news, ls, ms = [], [], [], []
    for e in range(per_step):
        m = jnp.maximum(jnp.max(scs[e], axis=1, keepdims=True), sc_news[e])
        p = jnp.where(valid, jnp.exp(scs[e] - m), 0.0)
        p_new = jnp.exp(sc_news[e] - m)
        ps.append(p)
        p_news.append(p_new)
        ms.append(m)
        ls.append(jnp.sum(p, axis=1, keepdims=True) + p_new)

    for e in range(per_step):
        parts = []
        for hd in range(A_HEADS):
            part = [jnp.zeros((A_HEAD_DIM, LANES), F32)]

            def weigh(t, vt, hd=hd, part=part, e=e):
                part[0] = part[0] + vt * ps[e][hd:hd + 1, tiles[t]]

            stream(e, vrows[hd], weigh, newbs[e][vrows[hd], :])
            parts.append(part[0])
        acc = jnp.sum(jnp.concatenate(parts, axis=0), axis=1, keepdims=True)
        o = (acc + head_rows(p_news[e]) * newbs[e][A_WIDTH:, 0:1]) / head_rows(ls[e])
        onehot = lane_seq == seqs[e]
        ot_ref[...] = jnp.where(onehot, o, ot_ref[...])
        lset_ref[...] = jnp.where(onehot, ms[e] + jnp.log(ls[e]), lset_ref[...])


class _DecodeSide:
    def __init__(self, qt, newt, cache, layer, g, carry, first_seq, count, grid, lin):
        nl, bd, rows, lg = cache.shape
        per_step = DECODE_SEQS_PER_STEP[g]
        assert count % per_step == 0 and first_seq % per_step == 0
        n = count // per_step
        host_steps = 1
        for s in grid:
            host_steps *= s
        assert n <= host_steps
        kv_prev, ot_prev, lse_prev = carry
        const = lambda *idx: (0, 0)
        buf_spec = pl.BlockSpec(
            (None, per_step, rows, lg),
            lambda *idx: (layer, first_seq // per_step + jnp.minimum(lin(*idx), n - 1), 0, 0))
        ot_spec = pl.BlockSpec((A_WIDTH, bd), const)
        lse_spec = pl.BlockSpec((A_HEADS, bd), const)
        self.in_specs = [pl.BlockSpec((A_WIDTH, bd), const), pl.BlockSpec((rows, bd), const),
                         buf_spec, ot_spec, lse_spec]
        self.args = [qt, newt, cache, ot_prev, lse_prev]
        self.aliases = {}
        if kv_prev is not None:
            self.in_specs.append(pl.BlockSpec(memory_space=pl.ANY))
            self.args.append(kv_prev)
            self.aliases = {5: 0}
        self.out_specs = [buf_spec, ot_spec, lse_spec]
        self.out_shape = [jax.ShapeDtypeStruct(cache.shape, F32),
                          jax.ShapeDtypeStruct((A_WIDTH, bd), F32),
                          jax.ShapeDtypeStruct((A_HEADS, bd), F32)]

        self.scratch_shapes = []

        def body(ins, outs, scratch):
            qt_ref, newt_ref, c_ref, ot_in, lse_in = ins[:5]
            out_ref, ot_ref, lset_ref = outs
            step = lin(*[pl.program_id(a) for a in range(len(grid))])

            @pl.when(step == 0)
            def _():
                ot_ref[...] = ot_in[...]
                lset_ref[...] = lse_in[...]

            def work():
                _decode_attn_body(qt_ref, newt_ref, c_ref, out_ref, ot_ref, lset_ref,
                                  first_seq + step * per_step,
                                  dil=DILATIONS[g], n_rel=WINDOWS[g] // DILATIONS[g])

            if n < host_steps:
                pl.when(step < n)(work)
            else:
                work()

        self.body = body


def _call(host_kernel, grid, in_specs, out_specs, out_shape, scratch_shapes, args, name, side=None):
    n_in, n_out = len(in_specs), len(out_specs)
    if side is None:
        kernel, aliases, side_args = host_kernel, {}, []
    else:
        n_sin, n_sout = len(side.in_specs), len(side.out_specs)
        n_scr = len(scratch_shapes)

        def kernel(*refs):
            o0 = n_in + n_sin
            s0 = o0 + n_out + n_sout
            if host_kernel is not None:
                host_kernel(*refs[:n_in], *refs[o0:o0 + n_out], *refs[s0:s0 + n_scr])
            side.body(refs[n_in:o0], refs[o0 + n_out:s0], refs[s0 + n_scr:])

        aliases = {n_in + a: n_out + b for a, b in side.aliases.items()}
        in_specs, side_args = in_specs + side.in_specs, side.args
        out_specs, out_shape = out_specs + side.out_specs, out_shape + side.out_shape
        scratch_shapes = scratch_shapes + side.scratch_shapes
    outs = pl.pallas_call(
        kernel, grid=grid, in_specs=in_specs, out_specs=out_specs, out_shape=out_shape,
        scratch_shapes=scratch_shapes, input_output_aliases=aliases,
        compiler_params=_cparams(*(("arbitrary",) * len(grid))), name=name,
    )(*args, *side_args)
    return outs[:n_out], outs[n_out:]


class _Rider:
    def __init__(self, total, per_step, make, carry, name):
        self.total, self.per_step, self.make, self.carry, self.name = total, per_step, make, carry, name
        self.done = 0
        self.pending = False

    def ride(self, grid, lin=lambda i: i):
        steps = 1
        for s in grid:
            steps *= s
        count = min(steps * self.per_step, self.total - self.done)
        self.pending = count > 0
        if not self.pending:
            return None
        side = self.make(self.carry, self.done, count, grid, lin)
        self.done += count
        return side

    def update(self, rode):
        if self.pending:
            self.carry = tuple(rode)
            self.pending = False

    def finish(self):
        if self.done < self.total:
            grid = ((self.total - self.done) // self.per_step,)
            _, rode = _call(None, grid, [], [], [], [], [], self.name, self.ride(grid))
            self.update(rode)
        return self.carry


def _ffn_kernel(x_ref, g_ref, wg_ref, wu_ref, wd_ref, out_ref, *, chunk):
    x = x_ref[...]
    h = _rms(x, g_ref[...]).astype(BF16)
    out_ref[...] = x
    for c in range(wg_ref.shape[1] // chunk):
        cols = slice(c * chunk, (c + 1) * chunk)
        gate = _dot(h, wg_ref[:, cols])
        up = _dot(h, wu_ref[:, cols])
        out_ref[...] += _dot((_silu(gate) * up).astype(BF16), wd_ref[cols, :])


def _ffn(x, g, wg_all, wu_all, wd_all, layer, *, tm, side=None):
    m, d = x.shape
    f = wg_all.shape[2]
    assert f % FFN_CHUNK == 0
    row = lambda i: (i, 0)
    sel = lambda i: (layer, 0, 0)
    (out,), rode = _call(
        functools.partial(_ffn_kernel, chunk=FFN_CHUNK), (m // tm,),
        [pl.BlockSpec((tm, d), row), pl.BlockSpec((1, d), lambda i: (0, 0)),
         _resident((None, d, f), sel), _resident((None, d, f), sel), _resident((None, f, d), sel)],
        [pl.BlockSpec((tm, d), row)], [jax.ShapeDtypeStruct((m, d), F32)], [],
        [x, g, wg_all, wu_all, wd_all], "ffn", side)
    return out, rode


def _r_proj_kernel(x_ref, gmix_ref, w_ref, cos_ref, sin_ref, out_ref, *qkt_ref, qk_dim):
    h = _rms(x_ref[...], gmix_ref[...]).astype(BF16)
    half = qk_dim // 2
    nqk = 2 * R_HEADS * qk_dim
    cos, sin = cos_ref[...], sin_ref[...]
    p = _dot(h, w_ref[:, 0:nqk])
    for hd in range(2 * R_HEADS):
        scale = 1.0 if hd < R_HEADS else qk_dim ** -0.5
        c0 = hd * qk_dim
        x1, x2 = p[:, c0:c0 + half], p[:, c0 + half:c0 + qk_dim]
        o1 = ((x1 * cos - x2 * sin) * scale).astype(BF16)
        o2 = ((x2 * cos + x1 * sin) * scale).astype(BF16)
        out_ref[:, c0:c0 + half] = o1
        out_ref[:, c0 + half:c0 + qk_dim] = o2
        if qkt_ref:
            qkt_ref[0][c0:c0 + half, :] = o1.astype(F32).T.astype(BF16)
            qkt_ref[0][c0 + half:c0 + qk_dim, :] = o2.astype(F32).T.astype(BF16)
    for c0 in range(nqk, w_ref.shape[1], nqk):
        out_ref[:, c0:c0 + nqk] = _dot(h, w_ref[:, c0:c0 + nqk]).astype(BF16)


def _r_proj(x, gmix, w_all, layer, cos, sin, *, tm, transposed_qk=False, side=None):
    m, d = x.shape
    n = w_all.shape[2]
    qk_dim = n // (6 * R_HEADS)
    nqk = 2 * R_HEADS * qk_dim
    ntab = cos.shape[0] // tm
    tab = pl.BlockSpec((tm, qk_dim // 2), lambda i: (i % ntab, 0))
    out_specs = [pl.BlockSpec((tm, n), lambda i: (i, 0))]
    out_shape = [jax.ShapeDtypeStruct((m, n), BF16)]
    if transposed_qk:
        out_specs.append(pl.BlockSpec((nqk, tm), lambda i: (0, i)))
        out_shape.append(jax.ShapeDtypeStruct((nqk, m), BF16))
    return _call(
        functools.partial(_r_proj_kernel, qk_dim=qk_dim), (m // tm,),
        [pl.BlockSpec((tm, d), lambda i: (i, 0)),
         pl.BlockSpec((1, d), lambda i: (0, 0)),
         _resident((None, d, n), lambda i: (layer, 0, 0)), tab, tab],
        out_specs, out_shape, [], [x, gmix, w_all, cos, sin], "r_proj", side)


def _retention_kernel(p_ref, x_ref, dmask_ref, qdec_ref, kdec_ref, cdec_ref, gout_ref, wo_ref,
                      out_ref, state_ref, y_scr, *, qk, dv):
    c = pl.program_id(0)

    @pl.when(c == 0)
    def _():
        state_ref[...] = jnp.zeros_like(state_ref)

    v0 = 2 * R_HEADS * qk
    g0 = v0 + R_HEADS * dv
    for b in range(p_ref.shape[0]):
        for hd in range(R_HEADS):
            q = p_ref[b, :, hd * qk:(hd + 1) * qk]
            k = p_ref[b, :, (R_HEADS + hd) * qk:(R_HEADS + hd + 1) * qk]
            v = p_ref[b, :, v0 + hd * dv:v0 + (hd + 1) * dv]
            gate = p_ref[b, :, g0 + hd * dv:g0 + (hd + 1) * dv].astype(F32)
            s_prev = state_ref[b, hd]
            inter = _dot(q, s_prev.astype(BF16)) * qdec_ref[hd]
            scores = _dot_nt(q, k) * dmask_ref[hd]
            o = inter + _dot(scores.astype(BF16), v)
            kd = (k.astype(F32) * kdec_ref[hd]).astype(BF16)
            state_ref[b, hd] = s_prev * cdec_ref[hd][:, 0:1] + _dot_tn(kd, v)
            y = _rms(o, gout_ref[hd]) * _silu(gate)
            y_scr[b, :, hd * dv:(hd + 1) * dv] = y.astype(BF16)
        out_ref[b] = x_ref[b] + _dot(y_scr[b], wo_ref[...])


def _retention_tables(chunk):
    lg = jnp.log1p(-jnp.exp2(-5.0 - jnp.arange(R_HEADS, dtype=F32)))
    idx = jnp.arange(chunk, dtype=F32)
    rel = idx[:, None] - idx[None, :]
    dmask = jnp.where(rel >= 0, jnp.exp(jnp.maximum(rel, 0.0) * lg[:, None, None]), 0.0)
    qdec = jnp.exp((idx + 1.0)[None, :] * lg[:, None])[:, :, None]
    kdec = jnp.exp((chunk - 1.0 - idx)[None, :] * lg[:, None])[:, :, None]
    cdec = jnp.broadcast_to(jnp.exp(chunk * lg)[:, None, None], (R_HEADS, 1, LANES))
    return dmask, qdec, kdec, cdec


def _retention(p, x, gout, wo_all, layer, batch, seq, *, chunk, side=None):
    m, n = p.shape
    d = x.shape[1]
    qk = n // (6 * R_HEADS)
    dv = 2 * qk
    tables = _retention_tables(chunk)
    full = lambda a: pl.BlockSpec(a.shape, lambda c: (0,) * a.ndim)
    rows = lambda width: pl.BlockSpec((batch, chunk, width), lambda c: (0, c, 0))
    (out, state), rode = _call(
        functools.partial(_retention_kernel, qk=qk, dv=dv), (seq // chunk,),
        [rows(n), rows(d)] + [full(t) for t in tables] + [
            full(gout), _resident((None, R_HEADS * dv, d), lambda c: (layer, 0, 0))],
        [rows(d), pl.BlockSpec((batch, R_HEADS, qk, dv), lambda c: (0, 0, 0, 0))],
        [jax.ShapeDtypeStruct((batch, seq, d), F32),
         jax.ShapeDtypeStruct((batch, R_HEADS, qk, dv), F32)],
        [pltpu.VMEM((batch, chunk, R_HEADS * dv), BF16)],
        [p.reshape(batch, seq, n), x.reshape(batch, seq, d), *tables, gout, wo_all],
        "retention", side)
    return out.reshape(m, d), state, rode


def _retention_step_body(p_ref, qkt_ref, s_ref, gout_ref, y_ref, snew_ref, first_seq, *, qk, dv):
    per_step = s_ref.shape[0]
    bd = qkt_ref.shape[1]
    row_seq = lax.broadcasted_iota(jnp.int32, (bd, LANES), 0)
    qkt = qkt_ref[...]
    v0 = 2 * R_HEADS * qk
    g0 = v0 + R_HEADS * dv
    for e in range(per_step):
        pick = (row_seq == first_seq + e).astype(BF16)
        qkb = _dot(qkt, pick)
        for hd in range(R_HEADS):
            gamma = 1.0 - 2.0 ** (-5 - hd)
            qb = qkb[hd * qk:(hd + 1) * qk, :]
            kb = qkb[(R_HEADS + hd) * qk:(R_HEADS + hd + 1) * qk, :]
            qk_dot = jnp.sum(qb * kb, axis=0, keepdims=True)
            o_tiles = []
            for t in range(dv // LANES):
                lanes = slice(t * LANES, (t + 1) * LANES)
                v = p_ref[e, :, v0 + hd * dv + t * LANES:v0 + hd * dv + (t + 1) * LANES].astype(F32)
                s_prev = s_ref[e, hd, :, lanes]
                inter = jnp.sum(s_prev * qb, axis=0, keepdims=True) * gamma
                o_tiles.append(inter + qk_dot * v)
                snew_ref[e, hd, :, lanes] = s_prev * gamma + kb * v
            o = jnp.concatenate(o_tiles, axis=1)
            gate = p_ref[e, :, g0 + hd * dv:g0 + (hd + 1) * dv].astype(F32)
            y = _rms(o, gout_ref[hd]) * _silu(gate)
            y_ref[e, :, hd * dv:(hd + 1) * dv] = y.astype(BF16)


class _RetentionSide:
    def __init__(self, p, qkt, state, layer, gout, carry, first_seq, count, grid, lin):
        bd = p.shape[0]
        nl, _, nh, qk, dv = state.shape
        per_step = RETENTION_SEQS_PER_STEP
        assert count % per_step == 0 and first_seq % per_step == 0
        n = count // per_step
        host_steps = 1
        for s in grid:
            host_steps *= s
        assert n <= host_steps
        blk = lambda *idx: first_seq // per_step + jnp.minimum(lin(*idx), n - 1)
        state_spec = pl.BlockSpec((None, per_step, nh, qk, dv),
                                  lambda *idx: (layer, blk(*idx), 0, 0, 0))
        y_spec = pl.BlockSpec((per_step, 1, nh * dv), lambda *idx: (blk(*idx), 0, 0))
        self.in_specs = [pl.BlockSpec((per_step, 1, p.shape[2]), lambda *idx: (blk(*idx), 0, 0)),
                         pl.BlockSpec(qkt.shape, lambda *idx: (0, 0)),
                         state_spec,
                         pl.BlockSpec((nh, 1, dv), lambda *idx: (0, 0, 0))]
        self.args = [p, qkt, state, gout]
        self.aliases = {}
        for out_idx, prev in enumerate(carry):
            if prev is not None:
                self.aliases[len(self.args)] = out_idx
                self.in_specs.append(pl.BlockSpec(memory_space=pl.ANY))
                self.args.append(prev)
        self.out_specs = [y_spec, state_spec]
        self.out_shape = [jax.ShapeDtypeStruct((bd, 1, nh * dv), BF16),
                          jax.ShapeDtypeStruct(state.shape, F32)]

        self.scratch_shapes = []

        def body(ins, outs, scratch):
            step = lin(*[pl.program_id(a) for a in range(len(grid))])

            def work():
                _retention_step_body(*ins[:4], *outs, first_seq + step * per_step, qk=qk, dv=dv)

            if n < host_steps:
                pl.when(step < n)(work)
            else:
                work()

        self.body = body


def _proj_res_kernel(y_ref, x_ref, w_ref, out_ref):
    out_ref[...] = x_ref[...] + _dot(y_ref[...], w_ref[...])


def _proj_res(y, x, w_all, layer, *, tm):
    m, d = x.shape
    kdim = y.shape[1]
    row = lambda i: (i, 0)
    return pl.pallas_call(
        _proj_res_kernel,
        grid=(m // tm,),
        in_specs=[pl.BlockSpec((tm, kdim), row), pl.BlockSpec((tm, d), row),
                  _resident((None, kdim, d), lambda i: (layer, 0, 0))],
        out_specs=pl.BlockSpec((tm, d), row),
        out_shape=jax.ShapeDtypeStruct((m, d), F32),
        compiler_params=_cparams("arbitrary"),
        name="proj_res",
    )(y, x, w_all)


def _rope_tables(pos, half, width, sign_fold):
    lane = np.arange(width)
    inv = np.float32(ROPE_THETA) ** (-(lane % half).astype(np.float32) / np.float32(half))
    ang = pos.astype(np.float32)[:, None] * inv[None, :]
    cos, sin = np.cos(ang), np.sin(ang)
    if sign_fold:
        sin = np.where((lane % (2 * half)) < half, -sin, sin)
    return jnp.asarray(cos, F32), jnp.asarray(sin, F32)


def kernel(x_prompt, x_sample, cache_kv_g0, cache_kv_g1, cache_kv_g2, state_retention,
           mix_norm, ffn_norm, a_w_qkv, a_q_norm, a_k_norm, a_w_o,
           r_w_in, r_out_norm, r_w_o, f_w_gate, f_w_up, f_w_down):
    batch, seq, d = x_prompt.shape
    bd, dec_seq, _ = x_sample.shape
    assert dec_seq == 1
    depth = mix_norm.shape[0]
    caches = (cache_kv_g0, cache_kv_g1, cache_kv_g2)
    keeps = tuple(min(w, seq) for w in WINDOWS)
    tm = ROW_TILE
    qk_dim = r_w_in.shape[2] // (6 * R_HEADS)

    xp = x_prompt.reshape(batch * seq, d)
    xs = x_sample.reshape(bd, d)

    caches_t = [jnp.transpose(c, (0, 1, 3, 4, 5, 2)).reshape(c.shape[0], bd, 2 * A_WIDTH, c.shape[2])
                for c in caches]

    a_w_qkv, a_w_o, r_w_in, r_w_o, f_w_gate, f_w_up, f_w_down = (
        w.astype(BF16) for w in (a_w_qkv, a_w_o, r_w_in, r_w_o, f_w_gate, f_w_up, f_w_down))

    pos_p = np.arange(seq)
    pos_s = np.full((bd,), PAST_LEN)
    a_tabs_p = _rope_tables(pos_p, A_HEAD_DIM // 2, LANES, True)
    a_tabs_s = _rope_tables(pos_s, A_HEAD_DIM // 2, LANES, True)
    r_tabs_p = _rope_tables(pos_p, qk_dim // 2, qk_dim // 2, False)
    r_tabs_s = _rope_tables(pos_s, qk_dim // 2, qk_dim // 2, False)
    lane = jnp.arange(MXU_DIM)
    ones_bd = (lane[:, None] // A_HEAD_DIM == lane[None, :] // A_HEAD_DIM).astype(BF16)
    no_dil = (1,) * N_GROUPS

    kv_p = [[] for _ in range(N_GROUPS)]
    kv_s = [None] * N_GROUPS
    ret_p, ret_s = [], None
    p_ahead = None
    for i in range(depth):
        j = i // 2
        gmix = mix_norm[i][None, :]
        if i % 2 == 0:
            gq = jnp.tile(a_q_norm[j], (1, A_HEADS))[:, None, :]
            gk = jnp.tile(a_k_norm[j], (1, A_HEADS))[:, None, :]
            qts, newts, _ = _a_proj(xs, gmix, a_w_qkv, j, gq, gk, *a_tabs_s, ones_bd, tm=bd, seq=bd,
                                    keeps=(bd,) * N_GROUPS, dils=no_dil, transposed_q=True)
            riders = [
                _Rider(bd, DECODE_SEQS_PER_STEP[g],
                       functools.partial(_DecodeSide, qts[g], newts[g][0], caches_t[g], j, g),
                       (kv_s[g], jnp.zeros((A_WIDTH, bd), F32), jnp.zeros((A_HEADS, bd), F32)),
                       f"decode_attn_g{g}")
                for g in range(N_GROUPS)]
            qkv, kvt, rode = _a_proj(xp, gmix, a_w_qkv, j, gq, gk, *a_tabs_p, ones_bd, tm=tm,
                                     seq=seq, keeps=keeps, dils=DILATIONS, transposed_q=False,
                                     side=riders[0].ride((batch * seq // tm,)))
            riders[0].update(rode)
            os_, lses = [], []
            for g in range(N_GROUPS):
                (o, lse), rode = _band_attn(*qkv[3 * g:3 * g + 3], g, batch, seq,
                                            riders[2].ride(*_band_attn_grid(g, batch, seq)))
                riders[2].update(rode)
                os_.append(o)
                lses.append(lse)
                kv_p[g].append(kvt[g])
            (xp,), rode = _a_out(os_, lses, xp, a_w_o, j, tm=A_OUT_TILE, dils=DILATIONS,
                                 side=riders[2].ride((batch * seq // A_OUT_TILE,)))
            riders[2].update(rode)
            ffn_rider = riders[1]

            def sample_mixer(xs):
                os_, lses = [], []
                for g in range(N_GROUPS):
                    kv_s[g], ot, lset = riders[g].finish()
                    os_.append(ot.T)
                    lses.append(jnp.repeat(lset.T, LSE_SEG, axis=1))
                (xs,), _ = _a_out(os_, lses, xs, a_w_o, j, tm=bd, dils=no_dil)
                return xs
        else:
            gout = r_out_norm[j][:, None, :]
            (ps, qkt), _ = _r_proj(xs, gmix, r_w_in, j, *r_tabs_s, tm=bd, transposed_qk=True)
            ffn_rider = _Rider(
                bd, RETENTION_SEQS_PER_STEP,
                functools.partial(_RetentionSide, ps[:, None, :], qkt, state_retention, j, gout),
                (None, ret_s), "retention_step")
            p = p_ahead
            if p is None:
                (p,), _ = _r_proj(xp, gmix, r_w_in, j, *r_tabs_p, tm=tm)
            xp, s_fin, rode = _retention(p, xp, gout, r_w_o, j, batch, seq, chunk=R_CHUNK,
                                         side=ffn_rider.ride((seq // R_CHUNK,)))
            ffn_rider.update(rode)
            ret_p.append(s_fin)

            def sample_mixer(xs):
                nonlocal ret_s
                ys, ret_s = ffn_rider.finish()
                return _proj_res(ys.reshape(bd, -1), xs, r_w_o, j, tm=bd)
        gffn = ffn_norm[i][None, :]
        xp, rode = _ffn(xp, gffn, f_w_gate, f_w_up, f_w_down, i, tm=tm,
                        side=ffn_rider.ride((batch * seq // tm,)))
        ffn_rider.update(rode)
        p_ahead = None
        if i % 2 == 0 and i + 1 < depth:
            (p_ahead,), rode = _r_proj(xp, mix_norm[i + 1][None, :], r_w_in, (i + 1) // 2, *r_tabs_p,
                                       tm=tm, side=ffn_rider.ride((batch * seq // tm,)))
            ffn_rider.update(rode)
        xs = sample_mixer(xs)
        xs, _ = _ffn(xs, gffn, f_w_gate, f_w_up, f_w_down, i, tm=bd)

    def rows_major(t):
        lead, slots = t.shape[:-2], t.shape[-1]
        t = t.reshape(lead + (2, A_HEADS, A_HEAD_DIM, slots))
        return jnp.moveaxis(t, -1, len(lead))

    return (xp.reshape(batch, seq, d), xs.reshape(bd, 1, d),
            rows_major(jnp.stack(kv_p[0])), rows_major(kv_s[0]),
            rows_major(jnp.stack(kv_p[1])), rows_major(kv_s[1]),
            rows_major(jnp.stack(kv_p[2])), rows_major(kv_s[2]),
            jnp.stack(ret_p), ret_s)
```

```python
import functools

import jax
import jax.numpy as jnp
import numpy as np
from jax import lax
from jax.experimental import pallas as pl
from jax.experimental.pallas import tpu as pltpu

F32 = jnp.float32
BF16 = jnp.bfloat16

EPS = 1e-6
ROPE_THETA = 10000.0
WINDOWS = (128, 512, 2048)
DILATIONS = (1, 4, 16)
N_GROUPS = 3
A_HEADS = 8
A_HEAD_DIM = 64
A_WIDTH = A_HEADS * A_HEAD_DIM
A_BLOCK = 128
R_HEADS = 4
R_CHUNK = 256
PAST_LEN = 2048
NEG = -1e30

VMEM_LIMIT_BYTES = 56 * 1024 * 1024
LANES = 128
MXU_DIM = 256
LSE_SEG = LANES // A_HEADS
ROW_TILE = 512
FFN_CHUNK = MXU_DIM
A_OUT_TILE = 512
DECODE_SEQS_PER_STEP = (4, 2, 1)
RETENTION_SEQS_PER_STEP = 2


def _cparams(*sem):
    return pltpu.CompilerParams(dimension_semantics=sem, vmem_limit_bytes=VMEM_LIMIT_BYTES)


def _resident(shape, index_map):
    return pl.BlockSpec(shape, index_map, pipeline_mode=pl.Buffered(1))


def _rms(x, g):
    ms = jnp.mean(x * x, axis=-1, keepdims=True)
    return x * lax.rsqrt(ms + EPS) * g


def _silu(x):
    return x * (1.0 / (1.0 + jnp.exp(-x)))


def _dot(a, b):
    return jnp.dot(a, b, preferred_element_type=F32)


def _dot_nt(a, b):
    return lax.dot_general(a, b, (((1,), (1,)), ((), ())), preferred_element_type=F32)


def _dot_tn(a, b):
    return lax.dot_general(a, b, (((0,), (0,)), ((), ())), preferred_element_type=F32)


def _a_proj_kernel(x_ref, gmix_ref, w_ref, gq_ref, gk_ref, cos_ref, sin_ref, ones_ref, *refs,
                   dils, tail_blocks, tiles_per_seq, transposed_q):
    n_qkv = N_GROUPS if transposed_q else 3 * N_GROUPS
    qkv_refs, kvt_refs, deint = refs[:n_qkv], refs[n_qkv:n_qkv + N_GROUPS], refs[n_qkv + N_GROUPS]
    tm = x_ref.shape[0]
    tile_in_seq = pl.program_id(0) % tiles_per_seq
    h = _rms(x_ref[...], gmix_ref[...]).astype(BF16)
    cos = jnp.concatenate([cos_ref[...]] * (A_WIDTH // LANES), axis=1)
    sin = jnp.concatenate([sin_ref[...]] * (A_WIDTH // LANES), axis=1)
    lane = lax.broadcasted_iota(jnp.int32, (1, A_WIDTH), 1)
    first_half = (lane % A_HEAD_DIM) < (A_HEAD_DIM // 2)
    ones = ones_ref[...]

    def normed_rope(p, gain):
        y = (p * p).astype(BF16)
        ss = jnp.concatenate([_dot(y[:, :MXU_DIM], ones), _dot(y[:, MXU_DIM:], ones)], axis=1)
        pn = p * lax.rsqrt(ss * (1.0 / A_HEAD_DIM) + EPS) * gain
        rot = jnp.where(first_half,
                        pltpu.roll(pn, A_WIDTH - A_HEAD_DIM // 2, 1),
                        pltpu.roll(pn, A_HEAD_DIM // 2, 1))
        return pn * cos + rot * sin

    def store_dilated(out_ref, slot, val, dil):
        if dil == 1:
            out_ref[...] = val.astype(BF16)
            return
        tiles = A_WIDTH // LANES
        for c in range(tiles):
            deint[slot * tiles + c] = val[:, c * LANES:(c + 1) * LANES]
        for r in range(dil):
            for c in range(tiles):
                out_ref[:, r * A_WIDTH + c * LANES:r * A_WIDTH + (c + 1) * LANES] = (
                    deint[slot * tiles + c, pl.ds(r, tm // dil, stride=dil), :].astype(BF16))

    for g in range(N_GROUPS):
        c0 = 3 * g * A_WIDTH
        q = normed_rope(_dot(h, w_ref[:, c0:c0 + A_WIDTH]), gq_ref[g]) * (A_HEAD_DIM ** -0.5)
        k = normed_rope(_dot(h, w_ref[:, c0 + A_WIDTH:c0 + 2 * A_WIDTH]), gk_ref[g])
        v = _dot(h, w_ref[:, c0 + 2 * A_WIDTH:c0 + 3 * A_WIDTH])
        if transposed_q:
            qkv_refs[g][...] = q.T.astype(BF16)
        else:
            store_dilated(qkv_refs[3 * g], 0, q, dils[g])
            store_dilated(qkv_refs[3 * g + 1], 1, k, dils[g])
            store_dilated(qkv_refs[3 * g + 2], 2, v, dils[g])

        kvt_ref = kvt_refs[g]
        tw = kvt_ref.shape[1]

        @pl.when(tile_in_seq >= tiles_per_seq - tail_blocks[g])
        def _(k=k, v=v, kvt_ref=kvt_ref, tw=tw):
            kvt_ref[0:A_WIDTH, :] = k[tm - tw:, :].T
            kvt_ref[A_WIDTH:, :] = v[tm - tw:, :].T


def _a_proj(x, gmix, w_all, layer, gq, gk, cos, sin, ones, *, tm, seq, keeps, dils, transposed_q,
            side=None):
    m, d = x.shape
    n_seq = m // seq
    tps = seq // tm
    ntab = cos.shape[0] // tm
    tws = [min(tm, kp) for kp in keeps]
    tail_blocks = tuple(kp // tw for kp, tw in zip(keeps, tws))

    def kvt_map(nb):
        return lambda i: (i // tps, 0, jnp.maximum(i % tps - (tps - nb), 0))

    if transposed_q:
        qkv_specs = [pl.BlockSpec((A_WIDTH, tm), lambda i: (0, i))] * N_GROUPS
        qkv_shapes = [jax.ShapeDtypeStruct((A_WIDTH, m), BF16)] * N_GROUPS
    else:
        qkv_specs, qkv_shapes = [], []
        for dil in dils:
            qkv_specs += [pl.BlockSpec((tm // dil, dil * A_WIDTH), lambda i: (i, 0))] * 3
            qkv_shapes += [jax.ShapeDtypeStruct((m // dil, dil * A_WIDTH), BF16)] * 3
    kvt_specs = [pl.BlockSpec((None, 2 * A_WIDTH, tw), kvt_map(nb)) for tw, nb in zip(tws, tail_blocks)]
    kvt_shapes = [jax.ShapeDtypeStruct((n_seq, 2 * A_WIDTH, kp), F32) for kp in keeps]
    gain_spec = pl.BlockSpec((N_GROUPS, 1, A_WIDTH), lambda i: (0, 0, 0))
    tab_spec = pl.BlockSpec((tm, LANES), lambda i: (i % ntab, 0))
    outs, rode = _call(
        functools.partial(_a_proj_kernel, dils=dils, tail_blocks=tail_blocks, tiles_per_seq=tps,
                          transposed_q=transposed_q),
        (m // tm,),
        [pl.BlockSpec((tm, d), lambda i: (i, 0)),
         pl.BlockSpec((1, d), lambda i: (0, 0)),
         _resident((None, d, 3 * N_GROUPS * A_WIDTH), lambda i: (layer, 0, 0)),
         gain_spec, gain_spec, tab_spec, tab_spec,
         pl.BlockSpec((MXU_DIM, MXU_DIM), lambda i: (0, 0))],
        qkv_specs + kvt_specs, qkv_shapes + kvt_shapes,
        [pltpu.VMEM((3 * A_WIDTH // LANES, tm, LANES), F32)],
        [x, gmix, w_all, gq, gk, cos, sin, ones], "a_proj", side)
    return outs[:-N_GROUPS], outs[-N_GROUPS:], rode


def _band_attn_kernel(q_ref, kp_ref, kc_ref, vp_ref, vc_ref, o_ref, lse_ref, kb, vb, *, nsub):
    i = pl.program_id(2)
    kb[0:A_BLOCK, :] = kp_ref[...]
    kb[A_BLOCK:, :] = kc_ref[...]
    vb[0:A_BLOCK, :] = vp_ref[...]
    vb[A_BLOCK:, :] = vc_ref[...]
    heads_per_half = MXU_DIM // A_HEAD_DIM
    stacked = (heads_per_half * A_BLOCK, 2 * A_BLOCK)
    qi = lax.broadcasted_iota(jnp.int32, stacked, 0) & (A_BLOCK - 1)
    kj = lax.broadcasted_iota(jnp.int32, stacked, 1)
    band = (kj >= qi) & (kj <= qi + A_BLOCK)
    band_first = band & ((kj >= A_BLOCK) | (i > 0))
    head_of_lane = lax.broadcasted_iota(jnp.int32, (1, MXU_DIM), 1) // A_HEAD_DIM

    seg_of_lane = lax.broadcasted_iota(jnp.int32, (1, LANES), 1) // LSE_SEG

    for s in range(nsub):
        valid = band_first if s == 0 else band
        rows = slice(s * A_BLOCK, (s + 1) * A_BLOCK)
        keys = slice(s * A_BLOCK, (s + 2) * A_BLOCK)
        lse_all = jnp.zeros((A_BLOCK, LANES), F32)
        for half in range(A_WIDTH // MXU_DIM):
            lanes = slice(half * MXU_DIM, (half + 1) * MXU_DIM)
            qh = q_ref[rows, lanes]
            qs = jnp.concatenate(
                [jnp.where(head_of_lane == hd, qh, jnp.zeros_like(qh))
                 for hd in range(heads_per_half)], axis=0)
            sc = _dot_nt(qs, kb[keys, lanes])
            sc = jnp.where(valid, sc, NEG)
            m = jnp.max(sc, axis=1, keepdims=True)
            p = jnp.exp(sc - m)
            l = jnp.sum(p, axis=1, keepdims=True)
            r = _dot(p.astype(BF16), vb[keys, lanes]) * (1.0 / l)
            lse = m + jnp.log(l)
            o_half = jnp.zeros((A_BLOCK, MXU_DIM), F32)
            for hd in range(heads_per_half):
                hrows = slice(hd * A_BLOCK, (hd + 1) * A_BLOCK)
                o_half = jnp.where(head_of_lane == hd, r[hrows, :], o_half)
                lse_all = jnp.where(seg_of_lane == half * heads_per_half + hd, lse[hrows, :], lse_all)
            o_ref[rows, lanes] = o_half.astype(BF16)
        lse_ref[rows, :] = lse_all


def _band_attn_grid(g, batch, seq):
    dil = DILATIONS[g]
    sub = seq // dil
    tq = min(4 * A_BLOCK, sub)
    nb = sub // tq
    return (batch, dil, nb), (lambda b, r, i: (b * dil + r) * nb + i)


def _band_attn(q, k, v, g, batch, seq, side=None):
    grid, _ = _band_attn_grid(g, batch, seq)
    nb = grid[2]
    tq = seq // DILATIONS[g] // nb
    nsub = tq // A_BLOCK
    cur = pl.BlockSpec((tq, A_WIDTH), lambda b, r, i: (b * nb + i, r))
    prev = pl.BlockSpec((A_BLOCK, A_WIDTH),
                        lambda b, r, i: (b * nb * nsub + jnp.maximum(i * nsub - 1, 0), r))
    lse_spec = pl.BlockSpec((tq, LANES), lambda b, r, i: (b * nb + i, r))
    return _call(
        functools.partial(_band_attn_kernel, nsub=nsub), grid,
        [cur, prev, cur, prev, cur], [cur, lse_spec],
        [jax.ShapeDtypeStruct(q.shape, BF16),
         jax.ShapeDtypeStruct((q.shape[0], DILATIONS[g] * LANES), F32)],
        [pltpu.VMEM((tq + A_BLOCK, A_WIDTH), BF16), pltpu.VMEM((tq + A_BLOCK, A_WIDTH), BF16)],
        [q, k, k, v, v], f"band_attn_g{g}", side)


def _a_out_kernel(o0_ref, o1_ref, o2_ref, l0_ref, l1_ref, l2_ref, x_ref, w_ref, spread_ref,
                  out_ref, scr, *, dils):
    tm = x_ref.shape[0]
    o_tiles = A_WIDTH // LANES

    def natural(ref, slot, dil, tiles):
        if dil == 1:
            return ref[...].astype(F32)
        for r in range(dil):
            for c in range(tiles):
                scr[slot + c, pl.ds(r, tm // dil, stride=dil), :] = (
                    ref[:, (r * tiles + c) * LANES:(r * tiles + c + 1) * LANES].astype(F32))
        if tiles == 1:
            return scr[slot]
        return jnp.concatenate([scr[slot + c] for c in range(tiles)], axis=1)

    def per_lane(lse):
        spread = spread_ref[...]
        hi, mid, lo = _split3(lse)
        return _dot(hi, spread) + _dot(mid, spread) + _dot(lo, spread)

    o0, o1, o2 = (natural(r, s * o_tiles, d, o_tiles)
                  for s, (r, d) in enumerate(zip((o0_ref, o1_ref, o2_ref), dils)))
    l0, l1, l2 = (per_lane(natural(r, N_GROUPS * o_tiles + s, d, 1))
                  for s, (r, d) in enumerate(zip((l0_ref, l1_ref, l2_ref), dils)))
    mx = jnp.maximum(jnp.maximum(l0, l1), l2)
    w0, w1, w2 = jnp.exp(l0 - mx), jnp.exp(l1 - mx), jnp.exp(l2 - mx)
    o = (w0 * o0 + w1 * o1 + w2 * o2) / (w0 + w1 + w2)
    out_ref[...] = x_ref[...] + _dot(o.astype(BF16), w_ref[...])


def _a_out(os_, lses, x, w_all, layer, *, tm, dils, side=None):
    m, d = x.shape
    row = lambda i: (i, 0)
    grp = lambda width: [pl.BlockSpec((tm // dil, dil * width), row) for dil in dils]
    lane = jnp.arange(LANES)[:, None]
    col = jnp.arange(A_WIDTH)[None, :]
    spread = (lane == LSE_SEG * (col // A_HEAD_DIM)).astype(BF16)
    return _call(
        functools.partial(_a_out_kernel, dils=dils), (m // tm,),
        grp(A_WIDTH) + grp(LANES) + [pl.BlockSpec((tm, d), row),
                                     _resident((None, A_WIDTH, d), lambda i: (layer, 0, 0)),
                                     pl.BlockSpec((LANES, A_WIDTH), lambda i: (0, 0))],
        [pl.BlockSpec((tm, d), row)], [jax.ShapeDtypeStruct((m, d), F32)],
        [pltpu.VMEM((N_GROUPS * (A_WIDTH // LANES + 1), tm, LANES), F32)],
        [*os_, *lses, x, w_all, spread], "a_out", side)


def _split3(x):
    hi = x.astype(BF16)
    r1 = x - hi.astype(F32)
    mid = r1.astype(BF16)
    lo = (r1 - mid.astype(F32)).astype(BF16)
    return hi, mid, lo


def _decode_attn_body(qt_ref, newt_ref, c_ref, out_ref, ot_ref, lset_ref, first_seq, *, dil, n_rel):
    per_step, _, lg = c_ref.shape
    bd = qt_ref.shape[1]
    reps = lg // LANES
    lane_seq = lax.broadcasted_iota(jnp.int32, (1, bd), 1)
    row_seq = lax.broadcasted_iota(jnp.int32, (bd, LANES), 0)
    head_row = lax.broadcasted_iota(jnp.int32, (A_HEADS, 1), 0)
    pos = lax.broadcasted_iota(jnp.int32, (1, lg), 1)
    back = lg - pos
    valid = ((back & (dil - 1)) == 0) & (back <= n_rel * dil)
    keep = lax.broadcasted_iota(jnp.int32, (1, LANES), 1) < LANES - 1
    tiles = [slice(t * LANES, (t + 1) * LANES) for t in range(reps)]

    def stream(e, rows, use, newcol):
        prev = None
        for t in range(reps):
            x = c_ref[e, rows, tiles[t]]
            use(t, x)
            rot = pltpu.roll(x, LANES - 1, 1)
            if prev is not None:
                out_ref[e, rows, tiles[t - 1]] = jnp.where(keep, prev, rot)
            prev = rot
        out_ref[e, rows, tiles[reps - 1]] = jnp.where(keep, prev, newcol)

    def head_rows(col):
        return jnp.concatenate(
            [jnp.broadcast_to(col[hd:hd + 1, :], (A_HEAD_DIM, 1)) for hd in range(A_HEADS)], axis=0)

    krows = [slice(hd * A_HEAD_DIM, (hd + 1) * A_HEAD_DIM) for hd in range(A_HEADS)]
    vrows = [slice(A_WIDTH + hd * A_HEAD_DIM, A_WIDTH + (hd + 1) * A_HEAD_DIM) for hd in range(A_HEADS)]
    seqs = [first_seq + e for e in range(per_step)]
    qt = qt_ref[...]
    new_hi, new_mid, new_lo = _split3(newt_ref[...])
    qbs, newbs = [], []
    for b in seqs:
        pick = (row_seq == b).astype(BF16)
        qbs.append(_dot(qt, pick))
        newbs.append(_dot(new_hi, pick) + _dot(new_mid, pick) + _dot(new_lo, pick))

    scs, sc_news = [], []
    for e in range(per_step):
        sc_tiles = [jnp.zeros((A_HEADS, LANES), F32) for _ in range(reps)]
        sc_new = jnp.zeros((A_HEADS, LANES), F32)
        for hd in range(A_HEADS):
            qh = qbs[e][krows[hd], :]

            def score(t, kt, hd=hd, qh=qh):
                sc_tiles[t] = jnp.where(head_row == hd,
                                        jnp.sum(kt * qh, axis=0, keepdims=True), sc_tiles[t])

            stream(e, krows[hd], score, newbs[e][krows[hd], :])
            sc_new = jnp.where(head_row == hd,
                               jnp.sum(newbs[e][krows[hd], :] * qh, axis=0, keepdims=True), sc_new)
        sc = sc_tiles[0] if reps == 1 else jnp.concatenate(sc_tiles, axis=1)
        scs.append(jnp.where(valid, sc, NEG))
        sc_news.append(sc_new[:, 0:1])

    ps, p_news, ls, ms = [], [], [], []
    for e in range(per_step):
        m = jnp.maximum(jnp.max(scs[e], axis=1, keepdims=True), sc_news[e])
        p = jnp.where(valid, jnp.exp(scs[e] - m), 0.0)
        p_new = jnp.exp(sc_news[e] - m)
        ps.append(p)
        p_news.append(p_new)
        ms.append(m)
        ls.append(jnp.sum(p, axis=1, keepdims=True) + p_new)

    for e in range(per_step):
        parts = []
        for hd in range(A_HEADS):
            part = [jnp.zeros((A_HEAD_DIM, LANES), F32)]

            def weigh(t, vt, hd=hd, part=part, e=e):
                part[0] = part[0] + vt * ps[e][hd:hd + 1, tiles[t]]

            stream(e, vrows[hd], weigh, newbs[e][vrows[hd], :])
            parts.append(part[0])
        acc = jnp.sum(jnp.concatenate(parts, axis=0), axis=1, keepdims=True)
        o = (acc + head_rows(p_news[e]) * newbs[e][A_WIDTH:, 0:1]) / head_rows(ls[e])
        onehot = lane_seq == seqs[e]
        ot_ref[...] = jnp.where(onehot, o, ot_ref[...])
        lset_ref[...] = jnp.where(onehot, ms[e] + jnp.log(ls[e]), lset_ref[...])


class _DecodeSide:
    def __init__(self, qt, newt, cache, layer, g, carry, first_seq, count, grid, lin):
        nl, bd, rows, lg = cache.shape
        per_step = DECODE_SEQS_PER_STEP[g]
        assert count % per_step == 0 and first_seq % per_step == 0
        n = count // per_step
        host_steps = 1
        for s in grid:
            host_steps *= s
        assert n <= host_steps
        kv_prev, ot_prev, lse_prev = carry
        const = lambda *idx: (0, 0)
        buf_spec = pl.BlockSpec(
            (None, per_step, rows, lg),
            lambda *idx: (layer, first_seq // per_step + jnp.minimum(lin(*idx), n - 1), 0, 0))
        ot_spec = pl.BlockSpec((A_WIDTH, bd), const)
        lse_spec = pl.BlockSpec((A_HEADS, bd), const)
        self.in_specs = [pl.BlockSpec((A_WIDTH, bd), const), pl.BlockSpec((rows, bd), const),
                         buf_spec, ot_spec, lse_spec]
        self.args = [qt, newt, cache, ot_prev, lse_prev]
        self.aliases = {}
        if kv_prev is not None:
            self.in_specs.append(pl.BlockSpec(memory_space=pl.ANY))
            self.args.append(kv_prev)
            self.aliases = {5: 0}
        self.out_specs = [buf_spec, ot_spec, lse_spec]
        self.out_shape = [jax.ShapeDtypeStruct(cache.shape, F32),
                          jax.ShapeDtypeStruct((A_WIDTH, bd), F32),
                          jax.ShapeDtypeStruct((A_HEADS, bd), F32)]

        self.scratch_shapes = []

        def body(ins, outs, scratch):
            qt_ref, newt_ref, c_ref, ot_in, lse_in = ins[:5]
            out_ref, ot_ref, lset_ref = outs
            step = lin(*[pl.program_id(a) for a in range(len(grid))])

            @pl.when(step == 0)
            def _():
                ot_ref[...] = ot_in[...]
                lset_ref[...] = lse_in[...]

            def work():
                _decode_attn_body(qt_ref, newt_ref, c_ref, out_ref, ot_ref, lset_ref,
                                  first_seq + step * per_step,
                                  dil=DILATIONS[g], n_rel=WINDOWS[g] // DILATIONS[g])

            if n < host_steps:
                pl.when(step < n)(work)
            else:
                work()

        self.body = body


def _call(host_kernel, grid, in_specs, out_specs, out_shape, scratch_shapes, args, name, side=None):
    n_in, n_out = len(in_specs), len(out_specs)
    if side is None:
        kernel, aliases, side_args = host_kernel, {}, []
    else:
        n_sin, n_sout = len(side.in_specs), len(side.out_specs)
        n_scr = len(scratch_shapes)

        def kernel(*refs):
            o0 = n_in + n_sin
            s0 = o0 + n_out + n_sout
            if host_kernel is not None:
                host_kernel(*refs[:n_in], *refs[o0:o0 + n_out], *refs[s0:s0 + n_scr])
            side.body(refs[n_in:o0], refs[o0 + n_out:s0], refs[s0 + n_scr:])

        aliases = {n_in + a: n_out + b for a, b in side.aliases.items()}
        in_specs, side_args = in_specs + side.in_specs, side.args
        out_specs, out_shape = out_specs + side.out_specs, out_shape + side.out_shape
        scratch_shapes = scratch_shapes + side.scratch_shapes
    outs = pl.pallas_call(
        kernel, grid=grid, in_specs=in_specs, out_specs=out_specs, out_shape=out_shape,
        scratch_shapes=scratch_shapes, input_output_aliases=aliases,
        compiler_params=_cparams(*(("arbitrary",) * len(grid))), name=name,
    )(*args, *side_args)
    return outs[:n_out], outs[n_out:]


class _Rider:
    def __init__(self, total, per_step, make, carry, name):
        self.total, self.per_step, self.make, self.carry, self.name = total, per_step, make, carry, name
        self.done = 0
        self.pending = False

    def ride(self, grid, lin=lambda i: i):
        steps = 1
        for s in grid:
            steps *= s
        count = min(steps * self.per_step, self.total - self.done)
        self.pending = count > 0
        if not self.pending:
            return None
        side = self.make(self.carry, self.done, count, grid, lin)
        self.done += count
        return side

    def update(self, rode):
        if self.pending:
            self.carry = tuple(rode)
            self.pending = False

    def finish(self):
        if self.done < self.total:
            grid = ((self.total - self.done) // self.per_step,)
            _, rode = _call(None, grid, [], [], [], [], [], self.name, self.ride(grid))
            self.update(rode)
        return self.carry


def _ffn_kernel(x_ref, g_ref, wg_ref, wu_ref, wd_ref, out_ref, *, chunk):
    x = x_ref[...]
    h = _rms(x, g_ref[...]).astype(BF16)
    out_ref[...] = x
    for c in range(wg_ref.shape[1] // chunk):
        cols = slice(c * chunk, (c + 1) * chunk)
        gate = _dot(h, wg_ref[:, cols])
        up = _dot(h, wu_ref[:, cols])
        out_ref[...] += _dot((_silu(gate) * up).astype(BF16), wd_ref[cols, :])


def _ffn(x, g, wg_all, wu_all, wd_all, layer, *, tm, side=None):
    m, d = x.shape
    f = wg_all.shape[2]
    assert f % FFN_CHUNK == 0
    row = lambda i: (i, 0)
    sel = lambda i: (layer, 0, 0)
    (out,), rode = _call(
        functools.partial(_ffn_kernel, chunk=FFN_CHUNK), (m // tm,),
        [pl.BlockSpec((tm, d), row), pl.BlockSpec((1, d), lambda i: (0, 0)),
         _resident((None, d, f), sel), _resident((None, d, f), sel), _resident((None, f, d), sel)],
        [pl.BlockSpec((tm, d), row)], [jax.ShapeDtypeStruct((m, d), F32)], [],
        [x, g, wg_all, wu_all, wd_all], "ffn", side)
    return out, rode


def _r_proj_kernel(x_ref, gmix_ref, w_ref, cos_ref, sin_ref, out_ref, *qkt_ref, qk_dim):
    h = _rms(x_ref[...], gmix_ref[...]).astype(BF16)
    half = qk_dim // 2
    nqk = 2 * R_HEADS * qk_dim
    cos, sin = cos_ref[...], sin_ref[...]
    p = _dot(h, w_ref[:, 0:nqk])
    for hd in range(2 * R_HEADS):
        scale = 1.0 if hd < R_HEADS else qk_dim ** -0.5
        c0 = hd * qk_dim
        x1, x2 = p[:, c0:c0 + half], p[:, c0 + half:c0 + qk_dim]
        o1 = ((x1 * cos - x2 * sin) * scale).astype(BF16)
        o2 = ((x2 * cos + x1 * sin) * scale).astype(BF16)
        out_ref[:, c0:c0 + half] = o1
        out_ref[:, c0 + half:c0 + qk_dim] = o2
        if qkt_ref:
            qkt_ref[0][c0:c0 + half, :] = o1.astype(F32).T.astype(BF16)
            qkt_ref[0][c0 + half:c0 + qk_dim, :] = o2.astype(F32).T.astype(BF16)
    for c0 in range(nqk, w_ref.shape[1], nqk):
        out_ref[:, c0:c0 + nqk] = _dot(h, w_ref[:, c0:c0 + nqk]).astype(BF16)


def _r_proj(x, gmix, w_all, layer, cos, sin, *, tm, transposed_qk=False, side=None):
    m, d = x.shape
    n = w_all.shape[2]
    qk_dim = n // (6 * R_HEADS)
    nqk = 2 * R_HEADS * qk_dim
    ntab = cos.shape[0] // tm
    tab = pl.BlockSpec((tm, qk_dim // 2), lambda i: (i % ntab, 0))
    out_specs = [pl.BlockSpec((tm, n), lambda i: (i, 0))]
    out_shape = [jax.ShapeDtypeStruct((m, n), BF16)]
    if transposed_qk:
        out_specs.append(pl.BlockSpec((nqk, tm), lambda i: (0, i)))
        out_shape.append(jax.ShapeDtypeStruct((nqk, m), BF16))
    return _call(
        functools.partial(_r_proj_kernel, qk_dim=qk_dim), (m // tm,),
        [pl.BlockSpec((tm, d), lambda i: (i, 0)),
         pl.BlockSpec((1, d), lambda i: (0, 0)),
         _resident((None, d, n), lambda i: (layer, 0, 0)), tab, tab],
        out_specs, out_shape, [], [x, gmix, w_all, cos, sin], "r_proj", side)


def _retention_kernel(p_ref, x_ref, dmask_ref, qdec_ref, kdec_ref, cdec_ref, gout_ref, wo_ref,
                      out_ref, state_ref, y_scr, *, qk, dv):
    c = pl.program_id(0)

    @pl.when(c == 0)
    def _():
        state_ref[...] = jnp.zeros_like(state_ref)

    v0 = 2 * R_HEADS * qk
    g0 = v0 + R_HEADS * dv
    for b in range(p_ref.shape[0]):
        for hd in range(R_HEADS):
            q = p_ref[b, :, hd * qk:(hd + 1) * qk]
            k = p_ref[b, :, (R_HEADS + hd) * qk:(R_HEADS + hd + 1) * qk]
            v = p_ref[b, :, v0 + hd * dv:v0 + (hd + 1) * dv]
            gate = p_ref[b, :, g0 + hd * dv:g0 + (hd + 1) * dv].astype(F32)
            s_prev = state_ref[b, hd]
            inter = _dot(q, s_prev.astype(BF16)) * qdec_ref[hd]
            scores = _dot_nt(q, k) * dmask_ref[hd]
            o = inter + _dot(scores.astype(BF16), v)
            kd = (k.astype(F32) * kdec_ref[hd]).astype(BF16)
            state_ref[b, hd] = s_prev * cdec_ref[hd][:, 0:1] + _dot_tn(kd, v)
            y = _rms(o, gout_ref[hd]) * _silu(gate)
            y_scr[b, :, hd * dv:(hd + 1) * dv] = y.astype(BF16)
        out_ref[b] = x_ref[b] + _dot(y_scr[b], wo_ref[...])


def _retention_tables(chunk):
    lg = jnp.log1p(-jnp.exp2(-5.0 - jnp.arange(R_HEADS, dtype=F32)))
    idx = jnp.arange(chunk, dtype=F32)
    rel = idx[:, None] - idx[None, :]
    dmask = jnp.where(rel >= 0, jnp.exp(jnp.maximum(rel, 0.0) * lg[:, None, None]), 0.0)
    qdec = jnp.exp((idx + 1.0)[None, :] * lg[:, None])[:, :, None]
    kdec = jnp.exp((chunk - 1.0 - idx)[None, :] * lg[:, None])[:, :, None]
    cdec = jnp.broadcast_to(jnp.exp(chunk * lg)[:, None, None], (R_HEADS, 1, LANES))
    return dmask, qdec, kdec, cdec


def _retention(p, x, gout, wo_all, layer, batch, seq, *, chunk, side=None):
    m, n = p.shape
    d = x.shape[1]
    qk = n // (6 * R_HEADS)
    dv = 2 * qk
    tables = _retention_tables(chunk)
    full = lambda a: pl.BlockSpec(a.shape, lambda c: (0,) * a.ndim)
    rows = lambda width: pl.BlockSpec((batch, chunk, width), lambda c: (0, c, 0))
    (out, state), rode = _call(
        functools.partial(_retention_kernel, qk=qk, dv=dv), (seq // chunk,),
        [rows(n), rows(d)] + [full(t) for t in tables] + [
            full(gout), _resident((None, R_HEADS * dv, d), lambda c: (layer, 0, 0))],
        [rows(d), pl.BlockSpec((batch, R_HEADS, qk, dv), lambda c: (0, 0, 0, 0))],
        [jax.ShapeDtypeStruct((batch, seq, d), F32),
         jax.ShapeDtypeStruct((batch, R_HEADS, qk, dv), F32)],
        [pltpu.VMEM((batch, chunk, R_HEADS * dv), BF16)],
        [p.reshape(batch, seq, n), x.reshape(batch, seq, d), *tables, gout, wo_all],
        "retention", side)
    return out.reshape(m, d), state, rode


def _retention_step_body(p_ref, qkt_ref, s_ref, gout_ref, y_ref, snew_ref, first_seq, *, qk, dv):
    per_step = s_ref.shape[0]
    bd = qkt_ref.shape[1]
    row_seq = lax.broadcasted_iota(jnp.int32, (bd, LANES), 0)
    qkt = qkt_ref[...]
    v0 = 2 * R_HEADS * qk
    g0 = v0 + R_HEADS * dv
    for e in range(per_step):
        pick = (row_seq == first_seq + e).astype(BF16)
        qkb = _dot(qkt, pick)
        for hd in range(R_HEADS):
            gamma = 1.0 - 2.0 ** (-5 - hd)
            qb = qkb[hd * qk:(hd + 1) * qk, :]
            kb = qkb[(R_HEADS + hd) * qk:(R_HEADS + hd + 1) * qk, :]
            qk_dot = jnp.sum(qb * kb, axis=0, keepdims=True)
            o_tiles = []
            for t in range(dv // LANES):
                lanes = slice(t * LANES, (t + 1) * LANES)
                v = p_ref[e, :, v0 + hd * dv + t * LANES:v0 + hd * dv + (t + 1) * LANES].astype(F32)
                s_prev = s_ref[e, hd, :, lanes]
                inter = jnp.sum(s_prev * qb, axis=0, keepdims=True) * gamma
                o_tiles.append(inter + qk_dot * v)
                snew_ref[e, hd, :, lanes] = s_prev * gamma + kb * v
            o = jnp.concatenate(o_tiles, axis=1)
            gate = p_ref[e, :, g0 + hd * dv:g0 + (hd + 1) * dv].astype(F32)
            y = _rms(o, gout_ref[hd]) * _silu(gate)
            y_ref[e, :, hd * dv:(hd + 1) * dv] = y.astype(BF16)


class _RetentionSide:
    def __init__(self, p, qkt, state, layer, gout, carry, first_seq, count, grid, lin):
        bd = p.shape[0]
        nl, _, nh, qk, dv = state.shape
        per_step = RETENTION_SEQS_PER_STEP
        assert count % per_step == 0 and first_seq % per_step == 0
        n = count // per_step
        host_steps = 1
        for s in grid:
            host_steps *= s
        assert n <= host_steps
        blk = lambda *idx: first_seq // per_step + jnp.minimum(lin(*idx), n - 1)
        state_spec = pl.BlockSpec((None, per_step, nh, qk, dv),
                                  lambda *idx: (layer, blk(*idx), 0, 0, 0))
        y_spec = pl.BlockSpec((per_step, 1, nh * dv), lambda *idx: (blk(*idx), 0, 0))
        self.in_specs = [pl.BlockSpec((per_step, 1, p.shape[2]), lambda *idx: (blk(*idx), 0, 0)),
                         pl.BlockSpec(qkt.shape, lambda *idx: (0, 0)),
                         state_spec,
                         pl.BlockSpec((nh, 1, dv), lambda *idx: (0, 0, 0))]
        self.args = [p, qkt, state, gout]
        self.aliases = {}
        for out_idx, prev in enumerate(carry):
            if prev is not None:
                self.aliases[len(self.args)] = out_idx
                self.in_specs.append(pl.BlockSpec(memory_space=pl.ANY))
                self.args.append(prev)
        self.out_specs = [y_spec, state_spec]
        self.out_shape = [jax.ShapeDtypeStruct((bd, 1, nh * dv), BF16),
                          jax.ShapeDtypeStruct(state.shape, F32)]

        self.scratch_shapes = []

        def body(ins, outs, scratch):
            step = lin(*[pl.program_id(a) for a in range(len(grid))])

            def work():
                _retention_step_body(*ins[:4], *outs, first_seq + step * per_step, qk=qk, dv=dv)

            if n < host_steps:
                pl.when(step < n)(work)
            else:
                work()

        self.body = body


def _proj_res_kernel(y_ref, x_ref, w_ref, out_ref):
    out_ref[...] = x_ref[...] + _dot(y_ref[...], w_ref[...])


def _proj_res(y, x, w_all, layer, *, tm):
    m, d = x.shape
    kdim = y.shape[1]
    row = lambda i: (i, 0)
    return pl.pallas_call(
        _proj_res_kernel,
        grid=(m // tm,),
        in_specs=[pl.BlockSpec((tm, kdim), row), pl.BlockSpec((tm, d), row),
                  _resident((None, kdim, d), lambda i: (layer, 0, 0))],
        out_specs=pl.BlockSpec((tm, d), row),
        out_shape=jax.ShapeDtypeStruct((m, d), F32),
        compiler_params=_cparams("arbitrary"),
        name="proj_res",
    )(y, x, w_all)


def _rope_tables(pos, half, width, sign_fold):
    lane = np.arange(width)
    inv = np.float32(ROPE_THETA) ** (-(lane % half).astype(np.float32) / np.float32(half))
    ang = pos.astype(np.float32)[:, None] * inv[None, :]
    cos, sin = np.cos(ang), np.sin(ang)
    if sign_fold:
        sin = np.where((lane % (2 * half)) < half, -sin, sin)
    return jnp.asarray(cos, F32), jnp.asarray(sin, F32)


def kernel(x_prompt, x_sample, cache_kv_g0, cache_kv_g1, cache_kv_g2, state_retention,
           mix_norm, ffn_norm, a_w_qkv, a_q_norm, a_k_norm, a_w_o,
           r_w_in, r_out_norm, r_w_o, f_w_gate, f_w_up, f_w_down):
    batch, seq, d = x_prompt.shape
    bd, dec_seq, _ = x_sample.shape
    assert dec_seq == 1
    depth = mix_norm.shape[0]
    caches = (cache_kv_g0, cache_kv_g1, cache_kv_g2)
    keeps = tuple(min(w, seq) for w in WINDOWS)
    tm = ROW_TILE
    qk_dim = r_w_in.shape[2] // (6 * R_HEADS)

    xp = x_prompt.reshape(batch * seq, d)
    xs = x_sample.reshape(bd, d)

    caches_t = [jnp.transpose(c, (0, 1, 3, 4, 5, 2)).reshape(c.shape[0], bd, 2 * A_WIDTH, c.shape[2])
                for c in caches]

    a_w_qkv, a_w_o, r_w_in, r_w_o, f_w_gate, f_w_up, f_w_down = (
        w.astype(BF16) for w in (a_w_qkv, a_w_o, r_w_in, r_w_o, f_w_gate, f_w_up, f_w_down))

    pos_p = np.arange(seq)
    pos_s = np.full((bd,), PAST_LEN)
    a_tabs_p = _rope_tables(pos_p, A_HEAD_DIM // 2, LANES, True)
    a_tabs_s = _rope_tables(pos_s, A_HEAD_DIM // 2, LANES, True)
    r_tabs_p = _rope_tables(pos_p, qk_dim // 2, qk_dim // 2, False)
    r_tabs_s = _rope_tables(pos_s, qk_dim // 2, qk_dim // 2, False)
    lane = jnp.arange(MXU_DIM)
    ones_bd = (lane[:, None] // A_HEAD_DIM == lane[None, :] // A_HEAD_DIM).astype(BF16)
    no_dil = (1,) * N_GROUPS

    kv_p = [[] for _ in range(N_GROUPS)]
    kv_s = [None] * N_GROUPS
    ret_p, ret_s = [], None
    p_ahead = None
    for i in range(depth):
        j = i // 2
        gmix = mix_norm[i][None, :]
        if i % 2 == 0:
            gq = jnp.tile(a_q_norm[j], (1, A_HEADS))[:, None, :]
            gk = jnp.tile(a_k_norm[j], (1, A_HEADS))[:, None, :]
            qts, newts, _ = _a_proj(xs, gmix, a_w_qkv, j, gq, gk, *a_tabs_s, ones_bd, tm=bd, seq=bd,
                                    keeps=(bd,) * N_GROUPS, dils=no_dil, transposed_q=True)
            riders = [
                _Rider(bd, DECODE_SEQS_PER_STEP[g],
                       functools.partial(_DecodeSide, qts[g], newts[g][0], caches_t[g], j, g),
                       (kv_s[g], jnp.zeros((A_WIDTH, bd), F32), jnp.zeros((A_HEADS, bd), F32)),
                       f"decode_attn_g{g}")
                for g in range(N_GROUPS)]
            qkv, kvt, rode = _a_proj(xp, gmix, a_w_qkv, j, gq, gk, *a_tabs_p, ones_bd, tm=tm,
                                     seq=seq, keeps=keeps, dils=DILATIONS, transposed_q=False,
                                     side=riders[0].ride((batch * seq // tm,)))
            riders[0].update(rode)
            os_, lses = [], []
            for g in range(N_GROUPS):
                (o, lse), rode = _band_attn(*qkv[3 * g:3 * g + 3], g, batch, seq,
                                            riders[2].ride(*_band_attn_grid(g, batch, seq)))
                riders[2].update(rode)
                os_.append(o)
                lses.append(lse)
                kv_p[g].append(kvt[g])
            (xp,), rode = _a_out(os_, lses, xp, a_w_o, j, tm=A_OUT_TILE, dils=DILATIONS,
                                 side=riders[2].ride((batch * seq // A_OUT_TILE,)))
            riders[2].update(rode)
            ffn_rider = riders[1]

            def sample_mixer(xs):
                os_, lses = [], []
                for g in range(N_GROUPS):
                    kv_s[g], ot, lset = riders[g].finish()
                    os_.append(ot.T)
                    lses.append(jnp.repeat(lset.T, LSE_SEG, axis=1))
                (xs,), _ = _a_out(os_, lses, xs, a_w_o, j, tm=bd, dils=no_dil)
                return xs
        else:
            gout = r_out_norm[j][:, None, :]
            (ps, qkt), _ = _r_proj(xs, gmix, r_w_in, j, *r_tabs_s, tm=bd, transposed_qk=True)
            ffn_rider = _Rider(
                bd, RETENTION_SEQS_PER_STEP,
                functools.partial(_RetentionSide, ps[:, None, :], qkt, state_retention, j, gout),
                (None, ret_s), "retention_step")
            p = p_ahead
            if p is None:
                (p,), _ = _r_proj(xp, gmix, r_w_in, j, *r_tabs_p, tm=tm)
            xp, s_fin, rode = _retention(p, xp, gout, r_w_o, j, batch, seq, chunk=R_CHUNK,
                                         side=ffn_rider.ride((seq // R_CHUNK,)))
            ffn_rider.update(rode)
            ret_p.append(s_fin)

            def sample_mixer(xs):
                nonlocal ret_s
                ys, ret_s = ffn_rider.finish()
                return _proj_res(ys.reshape(bd, -1), xs, r_w_o, j, tm=bd)
        gffn = ffn_norm[i][None, :]
        xp, rode = _ffn(xp, gffn, f_w_gate, f_w_up, f_w_down, i, tm=tm,
                        side=ffn_rider.ride((batch * seq // tm,)))
        ffn_rider.update(rode)
        p_ahead = None
        if i % 2 == 0 and i + 1 < depth:
            (p_ahead,), rode = _r_proj(xp, mix_norm[i + 1][None, :], r_w_in, (i + 1) // 2, *r_tabs_p,
                                       tm=tm, side=ffn_rider.ride((batch * seq // tm,)))
            ffn_rider.update(rode)
        xs = sample_mixer(xs)
        xs, _ = _ffn(xs, gffn, f_w_gate, f_w_up, f_w_down, i, tm=bd)

    def rows_major(t):
        lead, slots = t.shape[:-2], t.shape[-1]
        t = t.reshape(lead + (2, A_HEADS, A_HEAD_DIM, slots))
        return jnp.moveaxis(t, -1, len(lead))

    return (xp.reshape(batch, seq, d), xs.reshape(bd, 1, d),
            rows_major(jnp.stack(kv_p[0])), rows_major(kv_s[0]),
            rows_major(jnp.stack(kv_p[1])), rows_major(kv_s[1]),
            rows_major(jnp.stack(kv_p[2])), rows_major(kv_s[2]),
            jnp.stack(ret_p), ret_s)
```
